```python
import jax, jax.numpy as jnp
from jax import lax
import numpy as np

D_MODEL = 1024
BATCH = 16
SEQ = 2048
DEPTH = 2

D_FF = 2816
FFN_RES_WEIGHT = 0.5
A_WIDTH = D_MODEL // 2
A_CONV_WIDTH = 31
B_WIDTH = D_MODEL // 2
POOL_WINDOWS = (2, 4, 8, 16)
POOL_GROUPS = len(POOL_WINDOWS)
POOL_GROUP_DIM = B_WIDTH // POOL_GROUPS
EVEN_IN_WIDTH = 2 * A_WIDTH + B_WIDTH
C_WIDTH = D_MODEL // 2
SGU_CHUNK = 128
SGU_GROUPS = 4
SGU_GROUP_DIM = C_WIDTH // SGU_GROUPS
D_WIDTH = D_MODEL // 2
SHORTCONV_WIDTH = 3
ODD_IN_WIDTH = 2 * C_WIDTH + 3 * D_WIDTH
MIX_WIDTH = D_MODEL
EPS = 1e-6

kernel_name = "hybrid_conv_pool_sgu_shortconv_macaron"


def rmsnorm(x, g):
    xf = x.astype(jnp.float32)
    y = xf * lax.rsqrt(jnp.mean(xf * xf, axis=-1, keepdims=True) + EPS)
    return (y * g.astype(jnp.float32)).astype(x.dtype)


def layernorm(x, g, b):
    xf = x.astype(jnp.float32)
    mu = jnp.mean(xf, axis=-1, keepdims=True)
    var = jnp.mean(jnp.square(xf - mu), axis=-1, keepdims=True)
    y = (xf - mu) * lax.rsqrt(var + EPS)
    return (y * g.astype(jnp.float32) + b.astype(jnp.float32)).astype(x.dtype)


def swiglu(h, w_gate, w_up, w_down):
    return (jax.nn.silu(h @ w_gate) * (h @ w_up)) @ w_down


def causal_dwconv(x, w):
    k, c = w.shape
    return lax.conv_general_dilated(
        x, w[:, None, :].astype(x.dtype), window_strides=(1,), padding=[(k - 1, 0)],
        dimension_numbers=("NWC", "WIO", "NWC"), feature_group_count=c)


def multiscale_pool(v, pool_w, pool_scale):
    bsz, s, _ = v.shape
    cs = jnp.cumsum(v.astype(jnp.float32), axis=1)
    pos = jnp.arange(1, s + 1)
    means = []
    for gi, win in enumerate(POOL_WINDOWS):
        c = cs[..., gi * POOL_GROUP_DIM:(gi + 1) * POOL_GROUP_DIM]
        shifted = jnp.pad(c[:, :-win], ((0, 0), (win, 0), (0, 0)))
        cnt = jnp.minimum(pos, win).astype(jnp.float32)[:, None]
        means.append((c - shifted) / cnt)
    pooled = jnp.stack(means, axis=2).astype(v.dtype)
    diff = pooled - v.reshape(bsz, s, POOL_GROUPS, POOL_GROUP_DIM)
    out = jnp.einsum("bsgc,gcd->bsgd", diff, pool_w).reshape(bsz, s, B_WIDTH)
    return out * pool_scale


def even_mixer(h, w_in, conv_w, conv_b, ln_g, ln_b, pool_w, pool_scale, w_out):
    z = h @ w_in
    a_val, a_gate, b_in = jnp.split(z, [A_WIDTH, 2 * A_WIDTH], axis=-1)
    a = a_val * jax.nn.sigmoid(a_gate)
    a = causal_dwconv(a, conv_w) + conv_b
    a = jax.nn.silu(layernorm(a, ln_g, ln_b))
    b = multiscale_pool(b_in, pool_w, pool_scale)
    return jnp.concatenate([a, b], axis=-1) @ w_out


def odd_mixer(h, w_in, sgu_ln_g, sgu_ln_b, sgu_w, sgu_b, conv_w, w_out):
    bsz, s, _ = h.shape
    z = h @ w_in
    c_u, c_v, d_b, d_c, d_x = jnp.split(
        z, [C_WIDTH, 2 * C_WIDTH, 2 * C_WIDTH + D_WIDTH, 2 * C_WIDTH + 2 * D_WIDTH], axis=-1)
    c_u = jax.nn.gelu(c_u)
    c_v = layernorm(jax.nn.gelu(c_v), sgu_ln_g, sgu_ln_b)
    mask = jnp.tril(jnp.ones((SGU_CHUNK, SGU_CHUNK), dtype=bool))
    w_s = jnp.where(mask[None], sgu_w, jnp.zeros((), sgu_w.dtype))
    vc = c_v.reshape(bsz, s // SGU_CHUNK, SGU_CHUNK, SGU_GROUPS, SGU_GROUP_DIM)
    mixed = jnp.einsum("gts,bnsgc->bntgc", w_s, vc) + sgu_b.T[:, :, None]
    c_out = c_u * mixed.reshape(bsz, s, C_WIDTH)
    d_out = d_b * causal_dwconv(d_c * d_x, conv_w)
    return jnp.concatenate([c_out, d_out], axis=-1) @ w_out


def setup_inputs(seed: int = 0) -> dict:
    key = jax.random.key(seed)
    ks = iter(jax.random.split(key, 32))
    n_even = (DEPTH + 1) // 2
    n_odd = DEPTH // 2

    def nrm(shape, scale):
        return jax.random.normal(next(ks), shape, jnp.float32) * scale

    def gain(shape):
        return 1.0 + nrm(shape, 0.02)

    return {
        "x": nrm((BATCH, SEQ, D_MODEL), 1.0),
        "ffn1_norm": gain((DEPTH, D_MODEL)),
        "ffn1_w_gate": nrm((DEPTH, D_MODEL, D_FF), D_MODEL ** -0.5),
        "ffn1_w_up": nrm((DEPTH, D_MODEL, D_FF), D_MODEL ** -0.5),
        "ffn1_w_down": nrm((DEPTH, D_FF, D_MODEL), D_FF ** -0.5),
        "mix_norm": gain((DEPTH, D_MODEL)),
        "ffn2_norm": gain((DEPTH, D_MODEL)),
        "ffn2_w_gate": nrm((DEPTH, D_MODEL, D_FF), D_MODEL ** -0.5),
        "ffn2_w_up": nrm((DEPTH, D_MODEL, D_FF), D_MODEL ** -0.5),
        "ffn2_w_down": nrm((DEPTH, D_FF, D_MODEL), D_FF ** -0.5),
        "ev_w_in": nrm((n_even, D_MODEL, EVEN_IN_WIDTH), D_MODEL ** -0.5),
        "ev_conv_w": nrm((n_even, A_CONV_WIDTH, A_WIDTH), A_CONV_WIDTH ** -0.5),
        "ev_conv_b": nrm((n_even, A_WIDTH), 0.02),
        "ev_ln_g": gain((n_even, A_WIDTH)),
        "ev_ln_b": nrm((n_even, A_WIDTH), 0.02),
        "ev_pool_w": nrm((n_even, POOL_GROUPS, POOL_GROUP_DIM, POOL_GROUP_DIM), POOL_GROUP_DIM ** -0.5),
        "ev_pool_scale": 1.0 + nrm((n_even, B_WIDTH), 0.1),
        "ev_w_out": nrm((n_even, MIX_WIDTH, D_MODEL), MIX_WIDTH ** -0.5),
        "od_w_in": nrm((n_odd, D_MODEL, ODD_IN_WIDTH), D_MODEL ** -0.5),
        "od_sgu_ln_g": gain((n_odd, C_WIDTH)),
        "od_sgu_ln_b": nrm((n_odd, C_WIDTH), 0.02),
        "od_sgu_w": nrm((n_odd, SGU_GROUPS, SGU_CHUNK, SGU_CHUNK), SGU_CHUNK ** -0.5),
        "od_sgu_b": 1.0 + nrm((n_odd, SGU_GROUPS, SGU_CHUNK), 0.01),
        "od_conv_w": nrm((n_odd, SHORTCONV_WIDTH, D_WIDTH), SHORTCONV_WIDTH ** -0.5),
        "od_w_out": nrm((n_odd, MIX_WIDTH, D_MODEL), MIX_WIDTH ** -0.5),
        "final_norm": gain((D_MODEL,)),
    }


def reference(x, ffn1_norm, ffn1_w_gate, ffn1_w_up, ffn1_w_down, mix_norm,
              ffn2_norm, ffn2_w_gate, ffn2_w_up, ffn2_w_down,
              ev_w_in, ev_conv_w, ev_conv_b, ev_ln_g, ev_ln_b, ev_pool_w, ev_pool_scale, ev_w_out,
              od_w_in, od_sgu_ln_g, od_sgu_ln_b, od_sgu_w, od_sgu_b, od_conv_w, od_w_out,
              final_norm):
    for i in range(DEPTH):
        x = x + FFN_RES_WEIGHT * swiglu(rmsnorm(x, ffn1_norm[i]), ffn1_w_gate[i], ffn1_w_up[i], ffn1_w_down[i])
        h = rmsnorm(x, mix_norm[i])
        if i % 2 == 0:
            j = i // 2
            y = even_mixer(h, ev_w_in[j], ev_conv_w[j], ev_conv_b[j], ev_ln_g[j], ev_ln_b[j],
                           ev_pool_w[j], ev_pool_scale[j], ev_w_out[j])
        else:
            j = i // 2
            y = odd_mixer(h, od_w_in[j], od_sgu_ln_g[j], od_sgu_ln_b[j], od_sgu_w[j], od_sgu_b[j],
                          od_conv_w[j], od_w_out[j])
        x = x + y
        x = x + FFN_RES_WEIGHT * swiglu(rmsnorm(x, ffn2_norm[i]), ffn2_w_gate[i], ffn2_w_up[i], ffn2_w_down[i])
    return rmsnorm(x, final_norm)
```

```python
import functools

import jax
import jax.numpy as jnp
from jax import lax
from jax.experimental import pallas as pl
from jax.experimental.pallas import tpu as pltpu

EPS = 1e-6
FFN_RES_WEIGHT = 0.5
POOL_WINDOWS = (2, 4, 8, 16)
SGU_CHUNK = 128
LANES = 128
SUBLANES = 8
HALO = 32
VMEM_LIMIT_BYTES = 56 * 1024 * 1024

FFN_ROWS = 512
MIX_ROWS = 512
CONV_ROWS = 64

BF16 = jnp.bfloat16
F32 = jnp.float32


def _rmsnorm(x, g):
    return x * lax.rsqrt(jnp.mean(x * x, axis=-1, keepdims=True) + EPS) * g


def _layernorm(x, g, b):
    mu = jnp.mean(x, axis=-1, keepdims=True)
    xc = x - mu
    var = jnp.mean(xc * xc, axis=-1, keepdims=True)
    return xc * lax.rsqrt(var + EPS) * g + b


def _dot(a, b):
    return jnp.dot(a, b, preferred_element_type=F32)


def _resident(shape):
    zeros = (0,) * len(shape)
    return pl.BlockSpec(shape, lambda *_: zeros, pipeline_mode=pl.Buffered(1))


def _ffn_kernel(x_ref, g_ref, wg_ref, wu_ref, wd_ref, fn_ref, o_ref, *, final_norm):
    x = x_ref[...]
    h = _rmsnorm(x, g_ref[...]).astype(BF16)
    gate = _dot(h, wg_ref[...])
    up = _dot(h, wu_ref[...])
    act = (gate * jax.nn.sigmoid(gate) * up).astype(BF16)
    out = x + FFN_RES_WEIGHT * _dot(act, wd_ref[...])
    if final_norm:
        out = _rmsnorm(out, fn_ref[...])
    o_ref[...] = out


def _ffn(x2d, g, wg, wu, wd, fn, *, final_norm):
    n, d = x2d.shape
    dff = wg.shape[1]
    row_spec = pl.BlockSpec((FFN_ROWS, d), lambda i: (i, 0))
    return pl.pallas_call(
        functools.partial(_ffn_kernel, final_norm=final_norm),
        grid=(n // FFN_ROWS,),
        in_specs=[row_spec, _resident((1, d)), _resident((d, dff)), _resident((d, dff)),
                  _resident((dff, d)), _resident((1, d))],
        out_specs=row_spec,
        out_shape=jax.ShapeDtypeStruct((n, d), F32),
        compiler_params=pltpu.CompilerParams(
            dimension_semantics=("arbitrary",), vmem_limit_bytes=VMEM_LIMIT_BYTES),
        name="ffn_final" if final_norm else "ffn",
    )(x2d, g, wg, wu, wd, fn)


def _even_kernel(x_ref, g_ref, win_ref, cw_ref, cb_ref, lg_ref, lb_ref, pw_ref, ps_ref,
                 wout_ref, o_ref, abuf, bbuf, cbuf):
    rows = x_ref.shape[0]
    a_width = cw_ref.shape[1]
    taps = cw_ref.shape[0]
    n_groups, group_dim = pw_ref.shape[0], pw_ref.shape[1]

    @pl.when(pl.program_id(1) == 0)
    def _():
        abuf[0:HALO, :] = jnp.zeros((HALO, abuf.shape[1]), F32)
        bbuf[0:HALO, :] = jnp.zeros((HALO, bbuf.shape[1]), F32)

    x = x_ref[...]
    h = _rmsnorm(x, g_ref[...]).astype(BF16)
    z = _dot(h, win_ref[...])
    a_val = z[:, :a_width]
    a_gate = z[:, a_width:2 * a_width]
    abuf[HALO:HALO + rows, :] = a_val * jax.nn.sigmoid(a_gate)
    bbuf[HALO:HALO + rows, :] = z[:, 2 * a_width:]

    first = HALO - (taps - 1)

    def conv_block(r, carry):
        base = pl.multiple_of(r * CONV_ROWS, CONV_ROWS)
        for j in range(a_width // LANES):
            lanes = slice(j * LANES, (j + 1) * LANES)
            ext = abuf[pl.ds(base, CONV_ROWS + HALO), lanes]
            shifted = [ext] + [ext[r:r + CONV_ROWS + HALO - SUBLANES, :]
                               for r in range(1, SUBLANES)]
            acc = jnp.zeros((CONV_ROWS, LANES), F32)
            for k in range(taps):
                q, r = divmod(first + k, SUBLANES)
                acc = acc + (cw_ref[k:k + 1, lanes]
                             * shifted[r][q * SUBLANES:q * SUBLANES + CONV_ROWS, :])
            cbuf[pl.ds(base, CONV_ROWS), lanes] = acc
        return carry

    lax.fori_loop(0, rows // CONV_ROWS, conv_block, 0)
    a = _layernorm(cbuf[...] + cb_ref[...], lg_ref[...], lb_ref[...])
    a = a * jax.nn.sigmoid(a)

    pos = lax.broadcasted_iota(jnp.int32, (rows, 1), 0) + pl.program_id(1) * rows + 1
    b_parts = []
    for gi, win in enumerate(POOL_WINDOWS):
        lanes = slice(gi * group_dim, (gi + 1) * group_dim)
        ext = bbuf[HALO - win:HALO + rows, lanes]
        s = ext
        span = 1
        while span < win:
            s = s[span:, :] + s[:-span, :]
            span *= 2
        wsum = s[1:, :]
        cnt = jnp.minimum(pos, win).astype(F32)
        v = ext[win:, :]
        diff = (wsum / cnt - v).astype(BF16)
        b_parts.append(_dot(diff, pw_ref[gi]) * ps_ref[:, lanes])
    mixed = jnp.concatenate([a] + b_parts, axis=-1).astype(BF16)
    o_ref[...] = x + _dot(mixed, wout_ref[...])

    abuf[0:HALO, :] = abuf[rows:rows + HALO, :]
    bbuf[0:HALO, :] = bbuf[rows:rows + HALO, :]


def _even_mixer(x3d, g, w_in, conv_w, conv_b, ln_g, ln_b, pool_w, pool_scale, w_out):
    bsz, s, d = x3d.shape
    a_width = conv_w.shape[1]
    b_width = pool_scale.shape[1]
    row_spec = pl.BlockSpec((None, MIX_ROWS, d), lambda b, t: (b, t, 0))
    return pl.pallas_call(
        _even_kernel,
        grid=(bsz, s // MIX_ROWS),
        in_specs=[row_spec, _resident(g.shape), _resident(w_in.shape), _resident(conv_w.shape),
                  _resident(conv_b.shape), _resident(ln_g.shape), _resident(ln_b.shape),
                  _resident(pool_w.shape), _resident(pool_scale.shape), _resident(w_out.shape)],
        out_specs=row_spec,
        out_shape=jax.ShapeDtypeStruct((bsz, s, d), F32),
        scratch_shapes=[pltpu.VMEM((HALO + MIX_ROWS, a_width), F32),
                        pltpu.VMEM((HALO + MIX_ROWS, b_width), F32),
                        pltpu.VMEM((MIX_ROWS, a_width), F32)],
        compiler_params=pltpu.CompilerParams(
            dimension_semantics=("arbitrary", "arbitrary"), vmem_limit_bytes=VMEM_LIMIT_BYTES),
        name="even_mixer",
    )(x3d, g, w_in, conv_w, conv_b, ln_g, ln_b, pool_w, pool_scale, w_out)


def _gelu_tanh(x):
    c = 0.7978845608028654
    return 0.5 * x * (1.0 + jnp.tanh(c * (x + 0.044715 * (x * x * x))))


def _odd_kernel(x_ref, g_ref, win_ref, lg_ref, lb_ref, sw_ref, sb_ref, cw_ref, wout_ref,
                o_ref, dbuf):
    rows = x_ref.shape[0]
    c_width = lg_ref.shape[1]
    d_width = cw_ref.shape[1]
    taps = cw_ref.shape[0]
    n_groups = sw_ref.shape[0]
    group_dim = c_width // n_groups

    @pl.when(pl.program_id(1) == 0)
    def _():
        dbuf[0:HALO, :] = jnp.zeros((HALO, dbuf.shape[1]), F32)

    x = x_ref[...]
    h = _rmsnorm(x, g_ref[...]).astype(BF16)
    z = _dot(h, win_ref[...])
    c_u = _gelu_tanh(z[:, :c_width])
    c_v = _layernorm(_gelu_tanh(z[:, c_width:2 * c_width]), lg_ref[...], lb_ref[...]).astype(BF16)
    off = 2 * c_width
    d_b = z[:, off:off + d_width]
    dbuf[HALO:HALO + rows, :] = z[:, off + d_width:off + 2 * d_width] * z[:, off + 2 * d_width:]

    tri = (lax.broadcasted_iota(jnp.int32, (SGU_CHUNK, SGU_CHUNK), 0)
           >= lax.broadcasted_iota(jnp.int32, (SGU_CHUNK, SGU_CHUNK), 1))
    w_s = [jnp.where(tri, sw_ref[gi], 0.0).astype(BF16) for gi in range(n_groups)]
    chunk_rows = []
    for n in range(rows // SGU_CHUNK):
        parts = []
        for gi in range(n_groups):
            v = c_v[n * SGU_CHUNK:(n + 1) * SGU_CHUNK, gi * group_dim:(gi + 1) * group_dim]
            parts.append(_dot(w_s[gi], v) + sb_ref[:, gi:gi + 1])
        chunk_rows.append(jnp.concatenate(parts, axis=-1))
    c_out = c_u * jnp.concatenate(chunk_rows, axis=0)

    conv = jnp.zeros((rows, d_width), F32)
    for k in range(taps):
        start = HALO - (taps - 1) + k
        conv = conv + cw_ref[k:k + 1, :] * dbuf[start:start + rows, :]
    d_out = d_b * conv

    mixed = jnp.concatenate([c_out, d_out], axis=-1).astype(BF16)
    o_ref[...] = x + _dot(mixed, wout_ref[...])
    dbuf[0:HALO, :] = dbuf[rows:rows + HALO, :]


def _odd_mixer(x3d, g, w_in, ln_g, ln_b, sgu_w, sgu_b_t, conv_w, w_out):
    bsz, s, d = x3d.shape
    d_width = conv_w.shape[1]
    row_spec = pl.BlockSpec((None, MIX_ROWS, d), lambda b, t: (b, t, 0))
    return pl.pallas_call(
        _odd_kernel,
        grid=(bsz, s // MIX_ROWS),
        in_specs=[row_spec, _resident(g.shape), _resident(w_in.shape), _resident(ln_g.shape),
                  _resident(ln_b.shape), _resident(sgu_w.shape), _resident(sgu_b_t.shape),
                  _resident(conv_w.shape), _resident(w_out.shape)],
        out_specs=row_spec,
        out_shape=jax.ShapeDtypeStruct((bsz, s, d), F32),
        scratch_shapes=[pltpu.VMEM((HALO + MIX_ROWS, d_width), F32)],
        compiler_params=pltpu.CompilerParams(
            dimension_semantics=("arbitrary", "arbitrary"), vmem_limit_bytes=VMEM_LIMIT_BYTES),
        name="odd_mixer",
    )(x3d, g, w_in, ln_g, ln_b, sgu_w, sgu_b_t, conv_w, w_out)


def kernel(x, ffn1_norm, ffn1_w_gate, ffn1_w_up, ffn1_w_down, mix_norm, ffn2_norm, ffn2_w_gate, ffn2_w_up, ffn2_w_down, ev_w_in, ev_conv_w, ev_conv_b, ev_ln_g, ev_ln_b, ev_pool_w, ev_pool_scale, ev_w_out, od_w_in, od_sgu_ln_g, od_sgu_ln_b, od_sgu_w, od_sgu_b, od_conv_w, od_w_out, final_norm):
    bsz, s, d = x.shape
    depth = ffn1_norm.shape[0]
    assert s % MIX_ROWS == 0 and (bsz * s) % FFN_ROWS == 0 and MIX_ROWS % SGU_CHUNK == 0
    assert ev_conv_w.shape[1] - 1 <= HALO and max(POOL_WINDOWS) <= HALO

    def row(v):
        return v.reshape(1, -1)

    fn = row(final_norm)
    for i in range(depth):
        j = i // 2
        x2d = _ffn(x.reshape(bsz * s, d), row(ffn1_norm[i]), ffn1_w_gate[i].astype(BF16),
                   ffn1_w_up[i].astype(BF16), ffn1_w_down[i].astype(BF16), fn, final_norm=False)
        x = x2d.reshape(bsz, s, d)
        if i % 2 == 0:
            x = _even_mixer(x, row(mix_norm[i]), ev_w_in[j].astype(BF16), ev_conv_w[j],
                            row(ev_conv_b[j]), row(ev_ln_g[j]), row(ev_ln_b[j]),
                            ev_pool_w[j].astype(BF16), row(ev_pool_scale[j]),
                            ev_w_out[j].astype(BF16))
        else:
            x = _odd_mixer(x, row(mix_norm[i]), od_w_in[j].astype(BF16), row(od_sgu_ln_g[j]),
                           row(od_sgu_ln_b[j]), od_sgu_w[j], od_sgu_b[j].T, od_conv_w[j],
                           od_w_out[j].astype(BF16))
        x2d = _ffn(x.reshape(bsz * s, d), row(ffn2_norm[i]), ffn2_w_gate[i].astype(BF16),
                   ffn2_w_up[i].astype(BF16), ffn2_w_down[i].astype(BF16), fn,
                   final_norm=(i == depth - 1))
        x = x2d.reshape(bsz, s, d)
    return x
```

```python
import functools

import jax
import jax.numpy as jnp
from jax import lax
from jax.experimental import pallas as pl
from jax.experimental.pallas import tpu as pltpu

EPS = 1e-6
FFN_RES_WEIGHT = 0.5
POOL_WINDOWS = (2, 4, 8, 16)
SGU_CHUNK = 128
LANES = 128
SUBLANES = 8
BF16_ROWS = 2 * SUBLANES
HALO = 32
VMEM_LIMIT_BYTES = 56 * 1024 * 1024

FFN_ROWS = 512
MIX_ROWS = 512
CONV_ROWS = 64

BF16 = jnp.bfloat16
F32 = jnp.float32
U32 = jnp.uint32


def _rmsnorm(x, g):
    return x * lax.rsqrt(jnp.mean(x * x, axis=-1, keepdims=True) + EPS) * g


def _layernorm(x, g, b):
    mu = jnp.mean(x, axis=-1, keepdims=True)
    xc = x - mu
    var = jnp.mean(xc * xc, axis=-1, keepdims=True)
    return xc * lax.rsqrt(var + EPS) * g + b


def _dot(a, b):
    return jnp.dot(a, b, preferred_element_type=F32)


def _resident(shape):
    zeros = (0,) * len(shape)
    return pl.BlockSpec(shape, lambda *_: zeros, pipeline_mode=pl.Buffered(1))


def _ffn_kernel(x_ref, g_ref, wg_ref, wu_ref, wd_ref, fn_ref, o_ref, *, final_norm):
    x = x_ref[...]
    h = _rmsnorm(x, g_ref[...]).astype(BF16)
    gate = _dot(h, wg_ref[...])
    up = _dot(h, wu_ref[...])
    act = (gate * jax.nn.sigmoid(gate) * up).astype(BF16)
    out = x + FFN_RES_WEIGHT * _dot(act, wd_ref[...])
    if final_norm:
        out = _rmsnorm(out, fn_ref[...])
    o_ref[...] = out


def _layer_resident(shape, layer):
    index = (layer,) + (0,) * (len(shape) - 1)
    return pl.BlockSpec((None,) + tuple(shape[1:]), lambda *_: index,
                        pipeline_mode=pl.Buffered(1))


def _ffn(x2d, g, wg, wu, wd, fn, *, layer, final_norm):
    n, d = x2d.shape
    row_spec = pl.BlockSpec((FFN_ROWS, d), lambda i: (i, 0))
    return pl.pallas_call(
        functools.partial(_ffn_kernel, final_norm=final_norm),
        grid=(n // FFN_ROWS,),
        in_specs=[row_spec, _layer_resident(g.shape, layer), _layer_resident(wg.shape, layer),
                  _layer_resident(wu.shape, layer), _layer_resident(wd.shape, layer),
                  _resident(fn.shape)],
        out_specs=row_spec,
        out_shape=jax.ShapeDtypeStruct((n, d), F32),
        compiler_params=pltpu.CompilerParams(
            dimension_semantics=("arbitrary",), vmem_limit_bytes=VMEM_LIMIT_BYTES),
        name="ffn_final" if final_norm else "ffn",
    )(x2d, g, wg, wu, wd, fn)


def _even_kernel(x_ref, g_ref, win_ref, cw_ref, cb_ref, lg_ref, lb_ref, pw_ref, ps_ref,
                 wout_ref, o_ref, abuf, pk_even, pk_odd, bbuf, cbuf, dbuf):
    rows = x_ref.shape[0]
    taps, lane_tiles, pack_rows, _ = cw_ref.shape
    a_width = lane_tiles * LANES
    b_width = bbuf.shape[1]
    n_groups = len(POOL_WINDOWS)
    group_dim = b_width // n_groups
    ext_rows = HALO + rows

    @pl.when(pl.program_id(1) == 0)
    def _():
        abuf[0:HALO, :] = jnp.zeros((HALO, a_width), F32)
        abuf[ext_rows:ext_rows + SUBLANES, :] = jnp.zeros((SUBLANES, a_width), F32)
        bbuf[0:HALO, :] = jnp.zeros((HALO, b_width), F32)

    x = x_ref[...]
    h = _rmsnorm(x, g_ref[...]).astype(BF16)
    z = _dot(h, win_ref[...])
    a_val = z[:, :a_width]
    a_gate = z[:, a_width:2 * a_width]
    abuf[HALO:ext_rows, :] = a_val * jax.nn.sigmoid(a_gate)
    bbuf[HALO:ext_rows, :] = z[:, 2 * a_width:]

    for j in range(lane_tiles):
        lanes = slice(j * LANES, (j + 1) * LANES)
        pk_even[j] = pltpu.bitcast(abuf[0:ext_rows, lanes].astype(BF16), U32)
        pk_odd[j] = pltpu.bitcast(abuf[1:ext_rows + 1, lanes].astype(BF16), U32)
    first = HALO - (taps - 1)
    words = pack_rows // 2

    def conv_lane_tile(j, carry):
        for blk in range(rows // CONV_ROWS):
            accs = [jnp.zeros((pack_rows, LANES), F32) for _ in range(CONV_ROWS // pack_rows)]
            for k in range(taps):
                off = blk * CONV_ROWS + first + k
                src = pk_odd if off % 2 else pk_even
                wk = cw_ref[k, j].astype(F32)
                for gi in range(len(accs)):
                    w0 = off // 2 + gi * words
                    xk = pltpu.bitcast(src[j, w0:w0 + words, :], BF16)
                    accs[gi] = accs[gi] + xk.astype(F32) * wk
            for gi, acc in enumerate(accs):
                r0 = blk * CONV_ROWS + gi * pack_rows
                cbuf[j, r0:r0 + pack_rows, :] = acc
        return carry

    lax.fori_loop(0, lane_tiles, conv_lane_tile, 0)
    conv = jnp.concatenate([cbuf[j] for j in range(lane_tiles)], axis=-1)
    a = _layernorm(conv + cb_ref[...], lg_ref[...], lb_ref[...])
    a = a * jax.nn.sigmoid(a)

    pos = lax.broadcasted_iota(jnp.int32, (CONV_ROWS, LANES), 0) + pl.program_id(1) * rows + 1
    for gi, win in enumerate(POOL_WINDOWS):
        lanes = slice(gi * group_dim, (gi + 1) * group_dim)
        for blk in range(rows // CONV_ROWS):
            r0 = HALO + blk * CONV_ROWS
            v = bbuf[r0:r0 + CONV_ROWS, lanes]
            wsum = v
            for back in range(1, win):
                wsum = wsum + bbuf[r0 - back:r0 - back + CONV_ROWS, lanes]
            if blk == 0:
                mean = wsum / jnp.minimum(pos, win).astype(F32)
            else:
                mean = wsum * (1.0 / win)
            dbuf[blk * CONV_ROWS:(blk + 1) * CONV_ROWS, lanes] = (mean - v).astype(BF16)
    pair = 2 * group_dim
    b_parts = [_dot(dbuf[:, p * pair:(p + 1) * pair], pw_ref[p]) for p in range(n_groups // 2)]
    b = jnp.concatenate(b_parts, axis=-1) * ps_ref[...]
    mixed = jnp.concatenate([a, b], axis=-1).astype(BF16)
    o_ref[...] = x + _dot(mixed, wout_ref[...])

    abuf[0:HALO, :] = abuf[rows:ext_rows, :]
    bbuf[0:HALO, :] = bbuf[rows:ext_rows, :]


def _even_mixer(x3d, g, w_in, conv_w, conv_b, ln_g, ln_b, pool_w, pool_scale, w_out):
    bsz, s, d = x3d.shape
    taps, a_width = conv_w.shape
    b_width = pool_scale.shape[1]
    n_groups, group_dim, _ = pool_w.shape
    lane_tiles = a_width // LANES
    conv_wb = jnp.broadcast_to(conv_w.reshape(taps, lane_tiles, 1, LANES),
                               (taps, lane_tiles, BF16_ROWS, LANES)).astype(BF16)
    pw = pool_w.reshape(n_groups // 2, 2, group_dim, group_dim)
    zero = jnp.zeros_like(pw[:, 0])
    pool_wb = jnp.concatenate(
        [jnp.concatenate([pw[:, 0], zero], axis=-1), jnp.concatenate([zero, pw[:, 1]], axis=-1)],
        axis=-2).astype(BF16)
    row_spec = pl.BlockSpec((None, MIX_ROWS, d), lambda b, t: (b, t, 0))
    ext_rows = HALO + MIX_ROWS
    return pl.pallas_call(
        _even_kernel,
        grid=(bsz, s // MIX_ROWS),
        in_specs=[row_spec, _resident(g.shape), _resident(w_in.shape), _resident(conv_wb.shape),
                  _resident(conv_b.shape), _resident(ln_g.shape), _resident(ln_b.shape),
                  _resident(pool_wb.shape), _resident(pool_scale.shape), _resident(w_out.shape)],
        out_specs=row_spec,
        out_shape=jax.ShapeDtypeStruct((bsz, s, d), F32),
        scratch_shapes=[pltpu.VMEM((ext_rows + SUBLANES, a_width), F32),
                        pltpu.VMEM((lane_tiles, ext_rows // 2, LANES), U32),
                        pltpu.VMEM((lane_tiles, ext_rows // 2, LANES), U32),
                        pltpu.VMEM((ext_rows, b_width), F32),
                        pltpu.VMEM((lane_tiles, MIX_ROWS, LANES), F32),
                        pltpu.VMEM((MIX_ROWS, b_width), BF16)],
        compiler_params=pltpu.CompilerParams(
            dimension_semantics=("arbitrary", "arbitrary"), vmem_limit_bytes=VMEM_LIMIT_BYTES),
        name="even_mixer",
    )(x3d, g, w_in, conv_wb, conv_b, ln_g, ln_b, pool_wb, pool_scale, w_out)


def _gelu_tanh(x):
    c = 0.7978845608028654
    return 0.5 * x * (1.0 + jnp.tanh(c * (x + 0.044715 * (x * x * x))))


def _odd_kernel(x_ref, g_ref, win_ref, lg_ref, lb_ref, sw_ref, sb_ref, cw_ref, wout_ref,
                o_ref, dbuf):
    rows = x_ref.shape[0]
    c_width = lg_ref.shape[1]
    d_width = cw_ref.shape[1]
    taps = cw_ref.shape[0]
    n_groups = sw_ref.shape[0]
    group_dim = c_width // n_groups

    @pl.when(pl.program_id(1) == 0)
    def _():
        dbuf[0:HALO, :] = jnp.zeros((HALO, dbuf.shape[1]), F32)

    x = x_ref[...]
    h = _rmsnorm(x, g_ref[...]).astype(BF16)
    z = _dot(h, win_ref[...])
    c_u = _gelu_tanh(z[:, :c_width])
    c_v = _layernorm(_gelu_tanh(z[:, c_width:2 * c_width]), lg_ref[...], lb_ref[...]).astype(BF16)
    off = 2 * c_width
    d_b = z[:, off:off + d_width]
    dbuf[HALO:HALO + rows, :] = z[:, off + d_width:off + 2 * d_width] * z[:, off + 2 * d_width:]

    tri = (lax.broadcasted_iota(jnp.int32, (SGU_CHUNK, SGU_CHUNK), 0)
           >= lax.broadcasted_iota(jnp.int32, (SGU_CHUNK, SGU_CHUNK), 1))
    w_s = [jnp.where(tri, sw_ref[gi], 0.0).astype(BF16) for gi in range(n_groups)]
    chunk_rows = []
    for n in range(rows // SGU_CHUNK):
        parts = []
        for gi in range(n_groups):
            v = c_v[n * SGU_CHUNK:(n + 1) * SGU_CHUNK, gi * group_dim:(gi + 1) * group_dim]
            parts.append(_dot(w_s[gi], v) + sb_ref[:, gi:gi + 1])
        chunk_rows.append(jnp.concatenate(parts, axis=-1))
    c_out = c_u * jnp.concatenate(chunk_rows, axis=0)

    conv = jnp.zeros((rows, d_width), F32)
    for k in range(taps):
        start = HALO - (taps - 1) + k
        conv = conv + cw_ref[k:k + 1, :] * dbuf[start:start + rows, :]
    d_out = d_b * conv

    mixed = jnp.concatenate([c_out, d_out], axis=-1).astype(BF16)
    o_ref[...] = x + _dot(mixed, wout_ref[...])
    dbuf[0:HALO, :] = dbuf[rows:rows + HALO, :]


def _odd_mixer(x3d, g, w_in, ln_g, ln_b, sgu_w, sgu_b_t, conv_w, w_out):
    bsz, s, d = x3d.shape
    d_width = conv_w.shape[1]
    row_spec = pl.BlockSpec((None, MIX_ROWS, d), lambda b, t: (b, t, 0))
    return pl.pallas_call(
        _odd_kernel,
        grid=(bsz, s // MIX_ROWS),
        in_specs=[row_spec, _resident(g.shape), _resident(w_in.shape), _resident(ln_g.shape),
                  _resident(ln_b.shape), _resident(sgu_w.shape), _resident(sgu_b_t.shape),
                  _resident(conv_w.shape), _resident(w_out.shape)],
        out_specs=row_spec,
        out_shape=jax.ShapeDtypeStruct((bsz, s, d), F32),
        scratch_shapes=[pltpu.VMEM((HALO + MIX_ROWS, d_width), F32)],
        compiler_params=pltpu.CompilerParams(
            dimension_semantics=("arbitrary", "arbitrary"), vmem_limit_bytes=VMEM_LIMIT_BYTES),
        name="odd_mixer",
    )(x3d, g, w_in, ln_g, ln_b, sgu_w, sgu_b_t, conv_w, w_out)


def kernel(x, ffn1_norm, ffn1_w_gate, ffn1_w_up, ffn1_w_down, mix_norm, ffn2_norm, ffn2_w_gate, ffn2_w_up, ffn2_w_down, ev_w_in, ev_conv_w, ev_conv_b, ev_ln_g, ev_ln_b, ev_pool_w, ev_pool_scale, ev_w_out, od_w_in, od_sgu_ln_g, od_sgu_ln_b, od_sgu_w, od_sgu_b, od_conv_w, od_w_out, final_norm):
    bsz, s, d = x.shape
    depth = ffn1_norm.shape[0]
    assert s % MIX_ROWS == 0 and (bsz * s) % FFN_ROWS == 0 and MIX_ROWS % SGU_CHUNK == 0
    assert ev_conv_w.shape[1] - 1 <= HALO and max(POOL_WINDOWS) <= HALO

    def row(v):
        return v.reshape(1, -1)

    fn = row(final_norm)
    ffn1 = (ffn1_norm[:, None, :], ffn1_w_gate.astype(BF16), ffn1_w_up.astype(BF16),
            ffn1_w_down.astype(BF16))
    ffn2 = (ffn2_norm[:, None, :], ffn2_w_gate.astype(BF16), ffn2_w_up.astype(BF16),
            ffn2_w_down.astype(BF16))
    for i in range(depth):
        j = i // 2
        x2d = _ffn(x.reshape(bsz * s, d), *ffn1, fn, layer=i, final_norm=False)
        x = x2d.reshape(bsz, s, d)
        if i % 2 == 0:
            x = _even_mixer(x, row(mix_norm[i]), ev_w_in[j].astype(BF16), ev_conv_w[j],
                            row(ev_conv_b[j]), row(ev_ln_g[j]), row(ev_ln_b[j]),
                            ev_pool_w[j], row(ev_pool_scale[j]),
                            ev_w_out[j].astype(BF16))
        else:
            x = _odd_mixer(x, row(mix_norm[i]), od_w_in[j].astype(BF16), row(od_sgu_ln_g[j]),
                           row(od_sgu_ln_b[j]), od_sgu_w[j], od_sgu_b[j].T, od_conv_w[j],
                           od_w_out[j].astype(BF16))
        x2d = _ffn(x.reshape(bsz * s, d), *ffn2, fn, layer=i, final_norm=(i == depth - 1))
        x = x2d.reshape(bsz, s, d)
    return x
```

```python
import functools

import jax
import jax.numpy as jnp
from jax import lax
from jax.experimental import pallas as pl
from jax.experimental.pallas import tpu as pltpu

EPS = 1e-6
FFN_RES_WEIGHT = 0.5
POOL_WINDOWS = (2, 4, 8, 16)
SGU_CHUNK = 128
LANES = 128
SUBLANES = 8
BF16_ROWS = 2 * SUBLANES
HALO = 32
VMEM_LIMIT_BYTES = 56 * 1024 * 1024

FFN_ROWS = 1024
FFN_SUB_ROWS = 128
MIX_ROWS = 1024
MIX_SUB_ROWS = 256
CONV_ROWS = 64

BF16 = jnp.bfloat16
F32 = jnp.float32
U32 = jnp.uint32


def _rmsnorm(x, g):
    return x * lax.rsqrt(jnp.mean(x * x, axis=-1, keepdims=True) + EPS) * g


def _layernorm(x, g, b):
    mu = jnp.mean(x, axis=-1, keepdims=True)
    xc = x - mu
    var = jnp.mean(xc * xc, axis=-1, keepdims=True)
    return xc * lax.rsqrt(var + EPS) * g + b


def _dot(a, b):
    return jnp.dot(a, b, preferred_element_type=F32)


def _resident(shape):
    zeros = (0,) * len(shape)
    return pl.BlockSpec(shape, lambda *_: zeros, pipeline_mode=pl.Buffered(1))


def _layer_resident(shape, layer):
    index = (layer,) + (0,) * (len(shape) - 1)
    return pl.BlockSpec((None,) + tuple(shape[1:]), lambda *_: index,
                        pipeline_mode=pl.Buffered(1))


def _ffn_kernel(x_ref, g_ref, wg_ref, wu_ref, wd_ref, fn_ref, o_ref, *, final_norm):
    for r0 in range(0, x_ref.shape[0], FFN_SUB_ROWS):
        rows = slice(r0, r0 + FFN_SUB_ROWS)
        x = x_ref[rows, :]
        h = _rmsnorm(x, g_ref[...]).astype(BF16)
        gate = _dot(h, wg_ref[...])
        up = _dot(h, wu_ref[...])
        act = (gate * jax.nn.sigmoid(gate) * up).astype(BF16)
        out = x + FFN_RES_WEIGHT * _dot(act, wd_ref[...])
        if final_norm:
            out = _rmsnorm(out, fn_ref[...])
        o_ref[rows, :] = out


def _ffn(x2d, g, wg, wu, wd, fn, *, layer, final_norm):
    n, d = x2d.shape
    row_spec = pl.BlockSpec((FFN_ROWS, d), lambda i: (i, 0))
    return pl.pallas_call(
        functools.partial(_ffn_kernel, final_norm=final_norm),
        grid=(n // FFN_ROWS,),
        in_specs=[row_spec, _layer_resident(g.shape, layer), _layer_resident(wg.shape, layer),
                  _layer_resident(wu.shape, layer), _layer_resident(wd.shape, layer),
                  _resident(fn.shape)],
        out_specs=row_spec,
        out_shape=jax.ShapeDtypeStruct((n, d), F32),
        compiler_params=pltpu.CompilerParams(
            dimension_semantics=("arbitrary",), vmem_limit_bytes=VMEM_LIMIT_BYTES),
        name="ffn_final" if final_norm else "ffn",
    )(x2d, g, wg, wu, wd, fn)


def _even_kernel(x_ref, g_ref, win_ref, cw_ref, cb_ref, lg_ref, lb_ref, pw_ref, ps_ref,
                 wout_ref, o_ref, abuf, pk_even, pk_odd, bbuf, cbuf, dbuf):
    rows = x_ref.shape[0]
    taps, lane_tiles, pack_rows, _ = cw_ref.shape
    a_width = lane_tiles * LANES
    b_width = bbuf.shape[1]
    n_groups = len(POOL_WINDOWS)
    group_dim = b_width // n_groups
    ext_rows = HALO + rows

    @pl.when(pl.program_id(1) == 0)
    def _():
        abuf[0:HALO, :] = jnp.zeros((HALO, a_width), F32)
        abuf[ext_rows:ext_rows + SUBLANES, :] = jnp.zeros((SUBLANES, a_width), F32)
        bbuf[0:HALO, :] = jnp.zeros((HALO, b_width), F32)

    for r0 in range(0, rows, MIX_SUB_ROWS):
        h = _rmsnorm(x_ref[r0:r0 + MIX_SUB_ROWS, :], g_ref[...]).astype(BF16)
        z = _dot(h, win_ref[...])
        a_val = z[:, :a_width]
        a_gate = z[:, a_width:2 * a_width]
        abuf[HALO + r0:HALO + r0 + MIX_SUB_ROWS, :] = a_val * jax.nn.sigmoid(a_gate)
        bbuf[HALO + r0:HALO + r0 + MIX_SUB_ROWS, :] = z[:, 2 * a_width:]

    for j in range(lane_tiles):
        lanes = slice(j * LANES, (j + 1) * LANES)
        pk_even[j] = pltpu.bitcast(abuf[0:ext_rows, lanes].astype(BF16), U32)
        pk_odd[j] = pltpu.bitcast(abuf[1:ext_rows + 1, lanes].astype(BF16), U32)
    first = HALO - (taps - 1)
    words = pack_rows // 2

    def conv_lane_tile(j, carry):
        for blk in range(rows // CONV_ROWS):
            accs = [jnp.zeros((pack_rows, LANES), F32) for _ in range(CONV_ROWS // pack_rows)]
            for k in range(taps):
                off = blk * CONV_ROWS + first + k
                src = pk_odd if off % 2 else pk_even
                wk = cw_ref[k, j].astype(F32)
                for gi in range(len(accs)):
                    w0 = off // 2 + gi * words
                    xk = pltpu.bitcast(src[j, w0:w0 + words, :], BF16)
                    accs[gi] = accs[gi] + xk.astype(F32) * wk
            for gi, acc in enumerate(accs):
                q0 = blk * CONV_ROWS + gi * pack_rows
                cbuf[j, q0:q0 + pack_rows, :] = acc
        return carry

    lax.fori_loop(0, lane_tiles, conv_lane_tile, 0)

    pos = lax.broadcasted_iota(jnp.int32, (CONV_ROWS, LANES), 0) + pl.program_id(1) * rows + 1
    pair = 2 * group_dim
    for r0 in range(0, rows, MIX_SUB_ROWS):
        conv = jnp.concatenate([cbuf[j, r0:r0 + MIX_SUB_ROWS, :] for j in range(lane_tiles)],
                               axis=-1)
        a = _layernorm(conv + cb_ref[...], lg_ref[...], lb_ref[...])
        a = a * jax.nn.sigmoid(a)
        for gi, win in enumerate(POOL_WINDOWS):
            lanes = slice(gi * group_dim, (gi + 1) * group_dim)
            for q0 in range(r0, r0 + MIX_SUB_ROWS, CONV_ROWS):
                v = bbuf[HALO + q0:HALO + q0 + CONV_ROWS, lanes]
                wsum = v
                for back in range(1, win):
                    wsum = wsum + bbuf[HALO + q0 - back:HALO + q0 - back + CONV_ROWS, lanes]
                if q0 == 0:
                    mean = wsum / jnp.minimum(pos, win).astype(F32)
                else:
                    mean = wsum * (1.0 / win)
                dbuf[q0:q0 + CONV_ROWS, lanes] = (mean - v).astype(BF16)
        b_parts = [_dot(dbuf[r0:r0 + MIX_SUB_ROWS, p * pair:(p + 1) * pair], pw_ref[p])
                   for p in range(n_groups // 2)]
        b = jnp.concatenate(b_parts, axis=-1) * ps_ref[...]
        mixed = jnp.concatenate([a, b], axis=-1).astype(BF16)
        o_ref[r0:r0 + MIX_SUB_ROWS, :] = (x_ref[r0:r0 + MIX_SUB_ROWS, :]
                                          + _dot(mixed, wout_ref[...]))

    abuf[0:HALO, :] = abuf[rows:ext_rows, :]
    bbuf[0:HALO, :] = bbuf[rows:ext_rows, :]


def _even_mixer(x3d, g, w_in, conv_w, conv_b, ln_g, ln_b, pool_w, pool_scale, w_out):
    bsz, s, d = x3d.shape
    taps, a_width = conv_w.shape
    b_width = pool_scale.shape[1]
    n_groups, group_dim, _ = pool_w.shape
    lane_tiles = a_width // LANES
    conv_wb = jnp.broadcast_to(conv_w.reshape(taps, lane_tiles, 1, LANES),
                               (taps, lane_tiles, BF16_ROWS, LANES)).astype(BF16)
    pw = pool_w.reshape(n_groups // 2, 2, group_dim, group_dim)
    zero = jnp.zeros_like(pw[:, 0])
    pool_wb = jnp.concatenate(
        [jnp.concatenate([pw[:, 0], zero], axis=-1), jnp.concatenate([zero, pw[:, 1]], axis=-1)],
        axis=-2).astype(BF16)
    row_spec = pl.BlockSpec((None, MIX_ROWS, d), lambda b, t: (b, t, 0))
    ext_rows = HALO + MIX_ROWS
    return pl.pallas_call(
        _even_kernel,
        grid=(bsz, s // MIX_ROWS),
        in_specs=[row_spec, _resident(g.shape), _resident(w_in.shape), _resident(conv_wb.shape),
                  _resident(conv_b.shape), _resident(ln_g.shape), _resident(ln_b.shape),
                  _resident(pool_wb.shape), _resident(pool_scale.shape), _resident(w_out.shape)],
        out_specs=row_spec,
        out_shape=jax.ShapeDtypeStruct((bsz, s, d), F32),
        scratch_shapes=[pltpu.VMEM((ext_rows + SUBLANES, a_width), F32),
                        pltpu.VMEM((lane_tiles, ext_rows // 2, LANES), U32),
                        pltpu.VMEM((lane_tiles, ext_rows // 2, LANES), U32),
                        pltpu.VMEM((ext_rows, b_width), F32),
                        pltpu.VMEM((lane_tiles, MIX_ROWS, LANES), F32),
                        pltpu.VMEM((MIX_ROWS, b_width), BF16)],
        compiler_params=pltpu.CompilerParams(
            dimension_semantics=("arbitrary", "arbitrary"), vmem_limit_bytes=VMEM_LIMIT_BYTES),
        name="even_mixer",
    )(x3d, g, w_in, conv_wb, conv_b, ln_g, ln_b, pool_wb, pool_scale, w_out)


def _gelu_tanh(x):
    c = 0.7978845608028654
    return 0.5 * x * (1.0 + jnp.tanh(c * (x + 0.044715 * (x * x * x))))


def _odd_kernel(x_ref, g_ref, win_ref, lg_ref, lb_ref, sw_ref, sb_ref, cw_ref, wout_ref,
                o_ref, dbuf):
    rows = x_ref.shape[0]
    c_width = lg_ref.shape[1]
    d_width = cw_ref.shape[1]
    taps = cw_ref.shape[0]
    n_groups = sw_ref.shape[0]
    group_dim = c_width // n_groups

    @pl.when(pl.program_id(1) == 0)
    def _():
        dbuf[0:HALO, :] = jnp.zeros((HALO, dbuf.shape[1]), F32)

    tri = (lax.broadcasted_iota(jnp.int32, (SGU_CHUNK, SGU_CHUNK), 0)
           >= lax.broadcasted_iota(jnp.int32, (SGU_CHUNK, SGU_CHUNK), 1))
    w_s = [jnp.where(tri, sw_ref[gi], 0.0).astype(BF16) for gi in range(n_groups)]

    for r0 in range(0, rows, MIX_SUB_ROWS):
        x = x_ref[r0:r0 + MIX_SUB_ROWS, :]
        h = _rmsnorm(x, g_ref[...]).astype(BF16)
        z = _dot(h, win_ref[...])
        c_u = _gelu_tanh(z[:, :c_width])
        c_v = _layernorm(_gelu_tanh(z[:, c_width:2 * c_width]), lg_ref[...],
                         lb_ref[...]).astype(BF16)
        off = 2 * c_width
        d_b = z[:, off:off + d_width]
        dbuf[HALO + r0:HALO + r0 + MIX_SUB_ROWS, :] = (z[:, off + d_width:off + 2 * d_width]
                                                       * z[:, off + 2 * d_width:])

        chunk_rows = []
        for n in range(MIX_SUB_ROWS // SGU_CHUNK):
            parts = []
            for gi in range(n_groups):
                v = c_v[n * SGU_CHUNK:(n + 1) * SGU_CHUNK, gi * group_dim:(gi + 1) * group_dim]
                parts.append(_dot(w_s[gi], v) + sb_ref[:, gi:gi + 1])
            chunk_rows.append(jnp.concatenate(parts, axis=-1))
        c_out = c_u * jnp.concatenate(chunk_rows, axis=0)

        conv = jnp.zeros((MIX_SUB_ROWS, d_width), F32)
        for k in range(taps):
            start = HALO + r0 - (taps - 1) + k
            conv = conv + cw_ref[k:k + 1, :] * dbuf[start:start + MIX_SUB_ROWS, :]
        d_out = d_b * conv

        mixed = jnp.concatenate([c_out, d_out], axis=-1).astype(BF16)
        o_ref[r0:r0 + MIX_SUB_ROWS, :] = x + _dot(mixed, wout_ref[...])

    dbuf[0:HALO, :] = dbuf[rows:rows + HALO, :]


def _odd_mixer(x3d, g, w_in, ln_g, ln_b, sgu_w, sgu_b_t, conv_w, w_out):
    bsz, s, d = x3d.shape
    d_width = conv_w.shape[1]
    row_spec = pl.BlockSpec((None, MIX_ROWS, d), lambda b, t: (b, t, 0))
    return pl.pallas_call(
        _odd_kernel,
        grid=(bsz, s // MIX_ROWS),
        in_specs=[row_spec, _resident(g.shape), _resident(w_in.shape), _resident(ln_g.shape),
                  _resident(ln_b.shape), _resident(sgu_w.shape), _resident(sgu_b_t.shape),
                  _resident(conv_w.shape), _resident(w_out.shape)],
        out_specs=row_spec,
        out_shape=jax.ShapeDtypeStruct((bsz, s, d), F32),
        scratch_shapes=[pltpu.VMEM((HALO + MIX_ROWS, d_width), F32)],
        compiler_params=pltpu.CompilerParams(
            dimension_semantics=("arbitrary", "arbitrary"), vmem_limit_bytes=VMEM_LIMIT_BYTES),
        name="odd_mixer",
    )(x3d, g, w_in, ln_g, ln_b, sgu_w, sgu_b_t, conv_w, w_out)


def kernel(x, ffn1_norm, ffn1_w_gate, ffn1_w_up, ffn1_w_down, mix_norm, ffn2_norm, ffn2_w_gate, ffn2_w_up, ffn2_w_down, ev_w_in, ev_conv_w, ev_conv_b, ev_ln_g, ev_ln_b, ev_pool_w, ev_pool_scale, ev_w_out, od_w_in, od_sgu_ln_g, od_sgu_ln_b, od_sgu_w, od_sgu_b, od_conv_w, od_w_out, final_norm):
    bsz, s, d = x.shape
    depth = ffn1_norm.shape[0]
    assert s % MIX_ROWS == 0 and (bsz * s) % FFN_ROWS == 0
    assert FFN_ROWS % FFN_SUB_ROWS == 0 and MIX_ROWS % MIX_SUB_ROWS == 0
    assert MIX_SUB_ROWS % SGU_CHUNK == 0 and MIX_SUB_ROWS % CONV_ROWS == 0
    assert ev_conv_w.shape[1] - 1 <= HALO and max(POOL_WINDOWS) <= HALO

    def row(v):
        return v.reshape(1, -1)

    fn = row(final_norm)
    ffn1 = (ffn1_norm[:, None, :], ffn1_w_gate.astype(BF16), ffn1_w_up.astype(BF16),
            ffn1_w_down.astype(BF16))
    ffn2 = (ffn2_norm[:, None, :], ffn2_w_gate.astype(BF16), ffn2_w_up.astype(BF16),
            ffn2_w_down.astype(BF16))
    for i in range(depth):
        j = i // 2
        x2d = _ffn(x.reshape(bsz * s, d), *ffn1, fn, layer=i, final_norm=False)
        x = x2d.reshape(bsz, s, d)
        if i % 2 == 0:
            x = _even_mixer(x, row(mix_norm[i]), ev_w_in[j].astype(BF16), ev_conv_w[j],
                            row(ev_conv_b[j]), row(ev_ln_g[j]), row(ev_ln_b[j]),
                            ev_pool_w[j], row(ev_pool_scale[j]),
                            ev_w_out[j].astype(BF16))
        else:
            x = _odd_mixer(x, row(mix_norm[i]), od_w_in[j].astype(BF16), row(od_sgu_ln_g[j]),
                           row(od_sgu_ln_b[j]), od_sgu_w[j], od_sgu_b[j].T, od_conv_w[j],
                           od_w_out[j].astype(BF16))
        x2d = _ffn(x.reshape(bsz * s, d), *ffn2, fn, layer=i, final_norm=(i == depth - 1))
        x = x2d.reshape(bsz, s, d)
    return x
```

```python
import functools

import jax
import jax.numpy as jnp
from jax import lax
from jax.experimental import pallas as pl
from jax.experimental.pallas import tpu as pltpu

EPS = 1e-6
FFN_RES_WEIGHT = 0.5
POOL_WINDOWS = (2, 4, 8, 16)
SGU_CHUNK = 128
LANES = 128
SUBLANES = 8
BF16_ROWS = 2 * SUBLANES
HALO = 32
VMEM_LIMIT_BYTES = 56 * 1024 * 1024

FFN_ROWS = 1024
FFN_SUB_ROWS = 128
MIX_ROWS = 1024
MIX_SUB_ROWS = 256
CONV_ROWS = 64

BF16 = jnp.bfloat16
F32 = jnp.float32
U32 = jnp.uint32


def _rmsnorm(x, g):
    return x * lax.rsqrt(jnp.mean(x * x, axis=-1, keepdims=True) + EPS) * g


def _layernorm(x, g, b):
    mu = jnp.mean(x, axis=-1, keepdims=True)
    xc = x - mu
    var = jnp.mean(xc * xc, axis=-1, keepdims=True)
    return xc * lax.rsqrt(var + EPS) * g + b


def _dot(a, b):
    return jnp.dot(a, b, preferred_element_type=F32)


def _resident(shape):
    zeros = (0,) * len(shape)
    return pl.BlockSpec(shape, lambda *_: zeros, pipeline_mode=pl.Buffered(1))


def _layer_resident(shape, layer):
    index = (layer,) + (0,) * (len(shape) - 1)
    return pl.BlockSpec((None,) + tuple(shape[1:]), lambda *_: index,
                        pipeline_mode=pl.Buffered(1))


_HBM = pl.BlockSpec(memory_space=pl.ANY)


def _stage_chunk_rows(k):
    rows = k // 8
    assert rows * 8 == k and rows % SUBLANES == 0
    return rows


def _stage_scratch(k, n):
    return [pltpu.VMEM((k, n), BF16), pltpu.VMEM((2, _stage_chunk_rows(k), n), F32),
            pltpu.SemaphoreType.DMA((2,))]


def _load_weight(w_hbm, layer, w_vmem, stage, sem):
    k = w_vmem.shape[0]
    rows = stage.shape[1]

    def chunk_copy(c):
        slot = c % 2
        return pltpu.make_async_copy(w_hbm.at[layer, pl.ds(c * rows, rows), :],
                                     stage.at[slot], sem.at[slot])

    chunk_copy(0).start()
    for c in range(k // rows):
        if c + 1 < k // rows:
            chunk_copy(c + 1).start()
        chunk_copy(c).wait()
        w_vmem[c * rows:(c + 1) * rows, :] = stage[c % 2].astype(BF16)


def _ffn_kernel(x_ref, g_ref, wg_hbm, wu_hbm, wd_hbm, fn_ref, o_ref,
                wg_ref, wg_stage, wg_sem, wu_ref, wu_stage, wu_sem, wd_ref, wd_stage, wd_sem,
                *, layer, final_norm):
    @pl.when(pl.program_id(0) == 0)
    def _():
        _load_weight(wg_hbm, layer, wg_ref, wg_stage, wg_sem)
        _load_weight(wu_hbm, layer, wu_ref, wu_stage, wu_sem)
        _load_weight(wd_hbm, layer, wd_ref, wd_stage, wd_sem)

    for r0 in range(0, x_ref.shape[0], FFN_SUB_ROWS):
        rows = slice(r0, r0 + FFN_SUB_ROWS)
        x = x_ref[rows, :]
        h = _rmsnorm(x, g_ref[...]).astype(BF16)
        gate = _dot(h, wg_ref[...])
        up = _dot(h, wu_ref[...])
        act = (gate * jax.nn.sigmoid(gate) * up).astype(BF16)
        out = x + FFN_RES_WEIGHT * _dot(act, wd_ref[...])
        if final_norm:
            out = _rmsnorm(out, fn_ref[...])
        o_ref[rows, :] = out


def _ffn(x2d, g, wg, wu, wd, fn, *, layer, final_norm):
    n, d = x2d.shape
    dff = wg.shape[2]
    row_spec = pl.BlockSpec((FFN_ROWS, d), lambda i: (i, 0))
    return pl.pallas_call(
        functools.partial(_ffn_kernel, layer=layer, final_norm=final_norm),
        grid=(n // FFN_ROWS,),
        in_specs=[row_spec, _layer_resident(g.shape, layer), _HBM, _HBM, _HBM,
                  _resident(fn.shape)],
        out_specs=row_spec,
        out_shape=jax.ShapeDtypeStruct((n, d), F32),
        scratch_shapes=_stage_scratch(d, dff) + _stage_scratch(d, dff) + _stage_scratch(dff, d),
        compiler_params=pltpu.CompilerParams(
            dimension_semantics=("arbitrary",), vmem_limit_bytes=VMEM_LIMIT_BYTES),
        name="ffn_final" if final_norm else "ffn",
    )(x2d, g, wg, wu, wd, fn)


def _first_grid_step():
    return jnp.logical_and(pl.program_id(0) == 0, pl.program_id(1) == 0)


def _even_kernel(x_ref, g_ref, win_hbm, cw_ref, cb_ref, lg_ref, lb_ref, pw_ref, ps_ref,
                 wout_hbm, o_ref, win_ref, win_stage, win_sem, wout_ref, wout_stage, wout_sem,
                 abuf, pk_even, pk_odd, bbuf, cbuf, dbuf, *, layer):
    @pl.when(_first_grid_step())
    def _():
        _load_weight(win_hbm, layer, win_ref, win_stage, win_sem)
        _load_weight(wout_hbm, layer, wout_ref, wout_stage, wout_sem)

    rows = x_ref.shape[0]
    taps, lane_tiles, pack_rows, _ = cw_ref.shape
    a_width = lane_tiles * LANES
    n_groups, _, group_dim = bbuf.shape
    ext_rows = HALO + rows

    @pl.when(pl.program_id(1) == 0)
    def _():
        abuf[:, 0:HALO, :] = jnp.zeros((lane_tiles, HALO, LANES), F32)
        abuf[:, ext_rows:ext_rows + SUBLANES, :] = jnp.zeros((lane_tiles, SUBLANES, LANES), F32)
        bbuf[:, 0:HALO, :] = jnp.zeros((n_groups, HALO, group_dim), F32)

    for r0 in range(0, rows, MIX_SUB_ROWS):
        h = _rmsnorm(x_ref[r0:r0 + MIX_SUB_ROWS, :], g_ref[...]).astype(BF16)
        z = _dot(h, win_ref[...])
        a_val = z[:, :a_width]
        a_gate = z[:, a_width:2 * a_width]
        glu = a_val * jax.nn.sigmoid(a_gate)
        for j in range(lane_tiles):
            abuf[j, HALO + r0:HALO + r0 + MIX_SUB_ROWS, :] = glu[:, j * LANES:(j + 1) * LANES]
        for gi in range(n_groups):
            c0 = 2 * a_width + gi * group_dim
            bbuf[gi, HALO + r0:HALO + r0 + MIX_SUB_ROWS, :] = z[:, c0:c0 + group_dim]

    for j in range(lane_tiles):
        pk_even[j] = pltpu.bitcast(abuf[j, 0:ext_rows, :].astype(BF16), U32)
        pk_odd[j] = pltpu.bitcast(abuf[j, 1:ext_rows + 1, :].astype(BF16), U32)
    first = HALO - (taps - 1)
    words = pack_rows // 2

    def conv_lane_tile(j, carry):
        for blk in range(rows // CONV_ROWS):
            accs = [jnp.zeros((pack_rows, LANES), F32) for _ in range(CONV_ROWS // pack_rows)]
            for k in range(taps):
                off = blk * CONV_ROWS + first + k
                src = pk_odd if off % 2 else pk_even
                wk = cw_ref[k, j].astype(F32)
                for gi in range(len(accs)):
                    w0 = off // 2 + gi * words
                    xk = pltpu.bitcast(src[j, w0:w0 + words, :], BF16)
                    accs[gi] = accs[gi] + xk.astype(F32) * wk
            for gi, acc in enumerate(accs):
                q0 = blk * CONV_ROWS + gi * pack_rows
                cbuf[j, q0:q0 + pack_rows, :] = acc
        return carry

    lax.fori_loop(0, lane_tiles, conv_lane_tile, 0)

    pos = lax.broadcasted_iota(jnp.int32, (CONV_ROWS, LANES), 0) + pl.program_id(1) * rows + 1
    pair = 2 * group_dim
    for r0 in range(0, rows, MIX_SUB_ROWS):
        conv = jnp.concatenate([cbuf[j, r0:r0 + MIX_SUB_ROWS, :] for j in range(lane_tiles)],
                               axis=-1)
        a = _layernorm(conv + cb_ref[...], lg_ref[...], lb_ref[...])
        a = a * jax.nn.sigmoid(a)
        for gi, win in enumerate(POOL_WINDOWS):
            lanes = slice(gi * group_dim, (gi + 1) * group_dim)
            for q0 in range(r0, r0 + MIX_SUB_ROWS, CONV_ROWS):
                v = bbuf[gi, HALO + q0:HALO + q0 + CONV_ROWS, :]
                wsum = v
                for back in range(1, win):
                    wsum = wsum + bbuf[gi, HALO + q0 - back:HALO + q0 - back + CONV_ROWS, :]
                if q0 == 0:
                    mean = wsum / jnp.minimum(pos, win).astype(F32)
                else:
                    mean = wsum * (1.0 / win)
                dbuf[q0:q0 + CONV_ROWS, lanes] = (mean - v).astype(BF16)
        b_parts = [_dot(dbuf[r0:r0 + MIX_SUB_ROWS, p * pair:(p + 1) * pair], pw_ref[p])
                   for p in range(n_groups // 2)]
        b = jnp.concatenate(b_parts, axis=-1) * ps_ref[...]
        mixed = jnp.concatenate([a, b], axis=-1).astype(BF16)
        o_ref[r0:r0 + MIX_SUB_ROWS, :] = (x_ref[r0:r0 + MIX_SUB_ROWS, :]
                                          + _dot(mixed, wout_ref[...]))

    abuf[:, 0:HALO, :] = abuf[:, rows:ext_rows, :]
    bbuf[:, 0:HALO, :] = bbuf[:, rows:ext_rows, :]


def _even_mixer(x3d, g, w_in, conv_w, conv_b, ln_g, ln_b, pool_w, pool_scale, w_out, *, layer):
    bsz, s, d = x3d.shape
    taps, a_width = conv_w.shape
    b_width = pool_scale.shape[1]
    n_groups, group_dim, _ = pool_w.shape
    lane_tiles = a_width // LANES
    conv_wb = jnp.broadcast_to(conv_w.reshape(taps, lane_tiles, 1, LANES),
                               (taps, lane_tiles, BF16_ROWS, LANES)).astype(BF16)
    pw = pool_w.reshape(n_groups // 2, 2, group_dim, group_dim)
    zero = jnp.zeros_like(pw[:, 0])
    pool_wb = jnp.concatenate(
        [jnp.concatenate([pw[:, 0], zero], axis=-1), jnp.concatenate([zero, pw[:, 1]], axis=-1)],
        axis=-2).astype(BF16)
    row_spec = pl.BlockSpec((None, MIX_ROWS, d), lambda b, t: (b, t, 0))
    ext_rows = HALO + MIX_ROWS
    return pl.pallas_call(
        functools.partial(_even_kernel, layer=layer),
        grid=(bsz, s // MIX_ROWS),
        in_specs=[row_spec, _resident(g.shape), _HBM, _resident(conv_wb.shape),
                  _resident(conv_b.shape), _resident(ln_g.shape), _resident(ln_b.shape),
                  _resident(pool_wb.shape), _resident(pool_scale.shape), _HBM],
        out_specs=row_spec,
        out_shape=jax.ShapeDtypeStruct((bsz, s, d), F32),
        scratch_shapes=_stage_scratch(*w_in.shape[1:]) + _stage_scratch(*w_out.shape[1:]) + [
            pltpu.VMEM((lane_tiles, ext_rows + SUBLANES, LANES), F32),
            pltpu.VMEM((lane_tiles, ext_rows // 2, LANES), U32),
            pltpu.VMEM((lane_tiles, ext_rows // 2, LANES), U32),
            pltpu.VMEM((n_groups, ext_rows, group_dim), F32),
            pltpu.VMEM((lane_tiles, MIX_ROWS, LANES), F32),
            pltpu.VMEM((MIX_ROWS, b_width), BF16)],
        compiler_params=pltpu.CompilerParams(
            dimension_semantics=("arbitrary", "arbitrary"), vmem_limit_bytes=VMEM_LIMIT_BYTES),
        name="even_mixer",
    )(x3d, g, w_in, conv_wb, conv_b, ln_g, ln_b, pool_wb, pool_scale, w_out)


def _gelu_tanh(x):
    c = 0.7978845608028654
    return 0.5 * x * (1.0 + jnp.tanh(c * (x + 0.044715 * (x * x * x))))


def _odd_kernel(x_ref, g_ref, win_hbm, lg_ref, lb_ref, sw_ref, sb_ref, cw_ref, wout_hbm,
                o_ref, win_ref, win_stage, win_sem, wout_ref, wout_stage, wout_sem, dbuf,
                *, layer):
    @pl.when(_first_grid_step())
    def _():
        _load_weight(win_hbm, layer, win_ref, win_stage, win_sem)
        _load_weight(wout_hbm, layer, wout_ref, wout_stage, wout_sem)

    rows = x_ref.shape[0]
    c_width = lg_ref.shape[1]
    d_width = cw_ref.shape[1]
    taps = cw_ref.shape[0]
    n_groups = sw_ref.shape[0]
    group_dim = c_width // n_groups
    lane_tiles = d_width // LANES

    @pl.when(pl.program_id(1) == 0)
    def _():
        dbuf[:, 0:HALO, :] = jnp.zeros((lane_tiles, HALO, LANES), F32)

    tri = (lax.broadcasted_iota(jnp.int32, (SGU_CHUNK, SGU_CHUNK), 0)
           >= lax.broadcasted_iota(jnp.int32, (SGU_CHUNK, SGU_CHUNK), 1))
    w_s = [jnp.where(tri, sw_ref[gi], 0.0).astype(BF16) for gi in range(n_groups)]

    for r0 in range(0, rows, MIX_SUB_ROWS):
        x = x_ref[r0:r0 + MIX_SUB_ROWS, :]
        h = _rmsnorm(x, g_ref[...]).astype(BF16)
        z = _dot(h, win_ref[...])
        c_u = _gelu_tanh(z[:, :c_width])
        c_v = _layernorm(_gelu_tanh(z[:, c_width:2 * c_width]), lg_ref[...],
                         lb_ref[...]).astype(BF16)
        off = 2 * c_width
        d_b = z[:, off:off + d_width]
        d_cx = z[:, off + d_width:off + 2 * d_width] * z[:, off + 2 * d_width:]
        for j in range(lane_tiles):
            dbuf[j, HALO + r0:HALO + r0 + MIX_SUB_ROWS, :] = d_cx[:, j * LANES:(j + 1) * LANES]

        chunk_rows = []
        for n in range(MIX_SUB_ROWS // SGU_CHUNK):
            parts = []
            for gi in range(n_groups):
                v = c_v[n * SGU_CHUNK:(n + 1) * SGU_CHUNK, gi * group_dim:(gi + 1) * group_dim]
                parts.append(_dot(w_s[gi], v) + sb_ref[:, gi:gi + 1])
            chunk_rows.append(jnp.concatenate(parts, axis=-1))
        c_out = c_u * jnp.concatenate(chunk_rows, axis=0)

        conv_tiles = []
        for j in range(lane_tiles):
            conv = jnp.zeros((MIX_SUB_ROWS, LANES), F32)
            for k in range(taps):
                start = HALO + r0 - (taps - 1) + k
                conv = conv + (cw_ref[k:k + 1, j * LANES:(j + 1) * LANES]
                               * dbuf[j, start:start + MIX_SUB_ROWS, :])
            conv_tiles.append(conv)
        d_out = d_b * jnp.concatenate(conv_tiles, axis=-1)

        mixed = jnp.concatenate([c_out, d_out], axis=-1).astype(BF16)
        o_ref[r0:r0 + MIX_SUB_ROWS, :] = x + _dot(mixed, wout_ref[...])

    dbuf[:, 0:HALO, :] = dbuf[:, rows:rows + HALO, :]


def _odd_mixer(x3d, g, w_in, ln_g, ln_b, sgu_w, sgu_b_t, conv_w, w_out, *, layer):
    bsz, s, d = x3d.shape
    d_width = conv_w.shape[1]
    row_spec = pl.BlockSpec((None, MIX_ROWS, d), lambda b, t: (b, t, 0))
    return pl.pallas_call(
        functools.partial(_odd_kernel, layer=layer),
        grid=(bsz, s // MIX_ROWS),
        in_specs=[row_spec, _resident(g.shape), _HBM, _resident(ln_g.shape),
                  _resident(ln_b.shape), _resident(sgu_w.shape), _resident(sgu_b_t.shape),
                  _resident(conv_w.shape), _HBM],
        out_specs=row_spec,
        out_shape=jax.ShapeDtypeStruct((bsz, s, d), F32),
        scratch_shapes=_stage_scratch(*w_in.shape[1:]) + _stage_scratch(*w_out.shape[1:]) + [
            pltpu.VMEM((d_width // LANES, HALO + MIX_ROWS, LANES), F32)],
        compiler_params=pltpu.CompilerParams(
            dimension_semantics=("arbitrary", "arbitrary"), vmem_limit_bytes=VMEM_LIMIT_BYTES),
        name="odd_mixer",
    )(x3d, g, w_in, ln_g, ln_b, sgu_w, sgu_b_t, conv_w, w_out)


def kernel(x, ffn1_norm, ffn1_w_gate, ffn1_w_up, ffn1_w_down, mix_norm, ffn2_norm, ffn2_w_gate, ffn2_w_up, ffn2_w_down, ev_w_in, ev_conv_w, ev_conv_b, ev_ln_g, ev_ln_b, ev_pool_w, ev_pool_scale, ev_w_out, od_w_in, od_sgu_ln_g, od_sgu_ln_b, od_sgu_w, od_sgu_b, od_conv_w, od_w_out, final_norm):
    bsz, s, d = x.shape
    depth = ffn1_norm.shape[0]
    assert s % MIX_ROWS == 0 and (bsz * s) % FFN_ROWS == 0
    assert FFN_ROWS % FFN_SUB_ROWS == 0 and MIX_ROWS % MIX_SUB_ROWS == 0
    assert MIX_SUB_ROWS % SGU_CHUNK == 0 and MIX_SUB_ROWS % CONV_ROWS == 0
    assert ev_conv_w.shape[1] - 1 <= HALO and max(POOL_WINDOWS) <= HALO

    def row(v):
        return v.reshape(1, -1)

    fn = row(final_norm)
    ffn1 = (ffn1_norm[:, None, :], ffn1_w_gate, ffn1_w_up, ffn1_w_down)
    ffn2 = (ffn2_norm[:, None, :], ffn2_w_gate, ffn2_w_up, ffn2_w_down)
    for i in range(depth):
        j = i // 2
        x2d = _ffn(x.reshape(bsz * s, d), *ffn1, fn, layer=i, final_norm=False)
        x = x2d.reshape(bsz, s, d)
        if i % 2 == 0:
            x = _even_mixer(x, row(mix_norm[i]), ev_w_in, ev_conv_w[j], row(ev_conv_b[j]),
                            row(ev_ln_g[j]), row(ev_ln_b[j]), ev_pool_w[j],
                            row(ev_pool_scale[j]), ev_w_out, layer=j)
        else:
            x = _odd_mixer(x, row(mix_norm[i]), od_w_in, row(od_sgu_ln_g[j]),
                           row(od_sgu_ln_b[j]), od_sgu_w[j], od_sgu_b[j].T, od_conv_w[j],
                           od_w_out, layer=j)
        x2d = _ffn(x.reshape(bsz * s, d), *ffn2, fn, layer=i, final_norm=(i == depth - 1))
        x = x2d.reshape(bsz, s, d)
    return x
```

```python
import functools

import jax
import jax.numpy as jnp
from jax import lax
from jax.experimental import pallas as pl
from jax.experimental.pallas import tpu as pltpu

EPS = 1e-6
FFN_RES_WEIGHT = 0.5
POOL_WINDOWS = (2, 4, 8, 16)
SGU_CHUNK = 128
LANES = 128
SUBLANES = 8
BF16_ROWS = 2 * SUBLANES
HALO = 32
VMEM_LIMIT_BYTES = 56 * 1024 * 1024

FFN_ROWS = 1024
FFN_SUB_ROWS = 128
MIX_ROWS = 1024
MIX_SUB_ROWS = 256
CONV_ROWS = 64

BF16 = jnp.bfloat16
F32 = jnp.float32
U32 = jnp.uint32


def _rmsnorm(x, g):
    return x * lax.rsqrt(jnp.mean(x * x, axis=-1, keepdims=True) + EPS) * g


def _layernorm(x, g, b):
    mu = jnp.mean(x, axis=-1, keepdims=True)
    xc = x - mu
    var = jnp.mean(xc * xc, axis=-1, keepdims=True)
    return xc * lax.rsqrt(var + EPS) * g + b


def _dot(a, b):
    return jnp.dot(a, b, preferred_element_type=F32)


def _resident(shape):
    zeros = (0,) * len(shape)
    return pl.BlockSpec(shape, lambda *_: zeros, pipeline_mode=pl.Buffered(1))


def _layer_resident(shape, layer):
    index = (layer,) + (0,) * (len(shape) - 1)
    return pl.BlockSpec((None,) + tuple(shape[1:]), lambda *_: index,
                        pipeline_mode=pl.Buffered(1))


_HBM = pl.BlockSpec(memory_space=pl.ANY)


STAGE_CHUNKS = 8


def _stage_scratch(k, n):
    rows = k // STAGE_CHUNKS
    assert rows * STAGE_CHUNKS == k and rows % SUBLANES == 0
    return [pltpu.VMEM((k, n), BF16), pltpu.VMEM((2, rows, n), F32),
            pltpu.SemaphoreType.DMA((2,))]


def _load_weights(layer, *weights):
    def chunk_copy(w, c):
        w_hbm, _, stage, sem = w
        rows = stage.shape[1]
        return pltpu.make_async_copy(w_hbm.at[layer, pl.ds(c * rows, rows), :],
                                     stage.at[c % 2], sem.at[c % 2])

    for w in weights:
        chunk_copy(w, 0).start()
    for c in range(STAGE_CHUNKS):
        for w in weights:
            _, w_vmem, stage, _ = w
            rows = stage.shape[1]
            if c + 1 < STAGE_CHUNKS:
                chunk_copy(w, c + 1).start()
            chunk_copy(w, c).wait()
            w_vmem[c * rows:(c + 1) * rows, :] = stage[c % 2].astype(BF16)


def _ffn_kernel(x_ref, g_ref, wg_hbm, wu_hbm, wd_hbm, fn_ref, o_ref,
                wg_ref, wg_stage, wg_sem, wu_ref, wu_stage, wu_sem, wd_ref, wd_stage, wd_sem,
                *, layer, final_norm):
    @pl.when(pl.program_id(0) == 0)
    def _():
        _load_weights(layer, (wg_hbm, wg_ref, wg_stage, wg_sem),
                      (wu_hbm, wu_ref, wu_stage, wu_sem), (wd_hbm, wd_ref, wd_stage, wd_sem))

    for r0 in range(0, x_ref.shape[0], FFN_SUB_ROWS):
        rows = slice(r0, r0 + FFN_SUB_ROWS)
        x = x_ref[rows, :]
        h = _rmsnorm(x, g_ref[...]).astype(BF16)
        gate = _dot(h, wg_ref[...])
        up = _dot(h, wu_ref[...])
        act = (gate * jax.nn.sigmoid(gate) * up).astype(BF16)
        out = x + FFN_RES_WEIGHT * _dot(act, wd_ref[...])
        if final_norm:
            out = _rmsnorm(out, fn_ref[...])
        o_ref[rows, :] = out


def _ffn(x2d, g, wg, wu, wd, fn, *, layer, final_norm):
    n, d = x2d.shape
    dff = wg.shape[2]
    row_spec = pl.BlockSpec((FFN_ROWS, d), lambda i: (i, 0))
    return pl.pallas_call(
        functools.partial(_ffn_kernel, layer=layer, final_norm=final_norm),
        grid=(n // FFN_ROWS,),
        in_specs=[row_spec, _layer_resident(g.shape, layer), _HBM, _HBM, _HBM,
                  _resident(fn.shape)],
        out_specs=row_spec,
        out_shape=jax.ShapeDtypeStruct((n, d), F32),
        scratch_shapes=_stage_scratch(d, dff) + _stage_scratch(d, dff) + _stage_scratch(dff, d),
        compiler_params=pltpu.CompilerParams(
            dimension_semantics=("arbitrary",), vmem_limit_bytes=VMEM_LIMIT_BYTES),
        name="ffn_final" if final_norm else "ffn",
    )(x2d, g, wg, wu, wd, fn)


def _first_grid_step():
    return jnp.logical_and(pl.program_id(0) == 0, pl.program_id(1) == 0)


def _even_kernel(x_ref, g_ref, win_hbm, cw_ref, cb_ref, lg_ref, lb_ref, pw_ref, ps_ref,
                 wout_hbm, o_ref, win_ref, win_stage, win_sem, wout_ref, wout_stage, wout_sem,
                 abuf, pk_even, pk_odd, bbuf, cbuf, dbuf, *, layer):
    @pl.when(_first_grid_step())
    def _():
        _load_weights(layer, (win_hbm, win_ref, win_stage, win_sem),
                      (wout_hbm, wout_ref, wout_stage, wout_sem))

    rows = x_ref.shape[0]
    taps, lane_tiles, pack_rows, _ = cw_ref.shape
    a_width = lane_tiles * LANES
    n_groups, _, group_dim = bbuf.shape
    ext_rows = HALO + rows

    @pl.when(pl.program_id(1) == 0)
    def _():
        abuf[:, 0:HALO, :] = jnp.zeros((lane_tiles, HALO, LANES), F32)
        abuf[:, ext_rows:ext_rows + SUBLANES, :] = jnp.zeros((lane_tiles, SUBLANES, LANES), F32)
        bbuf[:, 0:HALO, :] = jnp.zeros((n_groups, HALO, group_dim), F32)

    pos = lax.broadcasted_iota(jnp.int32, (CONV_ROWS, LANES), 0) + pl.program_id(1) * rows + 1
    pair = 2 * group_dim
    for r0 in range(0, rows, MIX_SUB_ROWS):
        h = _rmsnorm(x_ref[r0:r0 + MIX_SUB_ROWS, :], g_ref[...]).astype(BF16)
        z = _dot(h, win_ref[...])
        a_val = z[:, :a_width]
        a_gate = z[:, a_width:2 * a_width]
        glu = a_val * jax.nn.sigmoid(a_gate)
        for j in range(lane_tiles):
            abuf[j, HALO + r0:HALO + r0 + MIX_SUB_ROWS, :] = glu[:, j * LANES:(j + 1) * LANES]
        for gi in range(n_groups):
            c0 = 2 * a_width + gi * group_dim
            bbuf[gi, HALO + r0:HALO + r0 + MIX_SUB_ROWS, :] = z[:, c0:c0 + group_dim]
        for gi, win in enumerate(POOL_WINDOWS):
            lanes = slice(gi * group_dim, (gi + 1) * group_dim)
            for q0 in range(r0, r0 + MIX_SUB_ROWS, CONV_ROWS):
                v = bbuf[gi, HALO + q0:HALO + q0 + CONV_ROWS, :]
                wsum = v
                for back in range(1, win):
                    wsum = wsum + bbuf[gi, HALO + q0 - back:HALO + q0 - back + CONV_ROWS, :]
                if q0 == 0:
                    mean = wsum / jnp.minimum(pos, win).astype(F32)
                else:
                    mean = wsum * (1.0 / win)
                dbuf[q0:q0 + CONV_ROWS, lanes] = (mean - v).astype(BF16)
        b_parts = [_dot(dbuf[r0:r0 + MIX_SUB_ROWS, p * pair:(p + 1) * pair], pw_ref[p])
                   for p in range(n_groups // 2)]
        b = jnp.concatenate(b_parts, axis=-1) * ps_ref[...]
        dbuf[r0:r0 + MIX_SUB_ROWS, :] = b.astype(BF16)

    for j in range(lane_tiles):
        pk_even[j] = pltpu.bitcast(abuf[j, 0:ext_rows, :].astype(BF16), U32)
        pk_odd[j] = pltpu.bitcast(abuf[j, 1:ext_rows + 1, :].astype(BF16), U32)
    first = HALO - (taps - 1)
    words = pack_rows // 2

    def conv_lane_tile(j, carry):
        for blk in range(rows // CONV_ROWS):
            accs = [jnp.zeros((pack_rows, LANES), F32) for _ in range(CONV_ROWS // pack_rows)]
            for k in range(taps):
                off = blk * CONV_ROWS + first + k
                src = pk_odd if off % 2 else pk_even
                wk = cw_ref[k, j].astype(F32)
                for gi in range(len(accs)):
                    w0 = off // 2 + gi * words
                    xk = pltpu.bitcast(src[j, w0:w0 + words, :], BF16)
                    accs[gi] = accs[gi] + xk.astype(F32) * wk
            for gi, acc in enumerate(accs):
                q0 = blk * CONV_ROWS + gi * pack_rows
                cbuf[j, q0:q0 + pack_rows, :] = acc
        return carry

    lax.fori_loop(0, lane_tiles, conv_lane_tile, 0)

    for r0 in range(0, rows, MIX_SUB_ROWS):
        conv = jnp.concatenate([cbuf[j, r0:r0 + MIX_SUB_ROWS, :] for j in range(lane_tiles)],
                               axis=-1)
        a = _layernorm(conv + cb_ref[...], lg_ref[...], lb_ref[...])
        a = a * jax.nn.sigmoid(a)
        mixed = jnp.concatenate([a.astype(BF16), dbuf[r0:r0 + MIX_SUB_ROWS, :]], axis=-1)
        o_ref[r0:r0 + MIX_SUB_ROWS, :] = (x_ref[r0:r0 + MIX_SUB_ROWS, :]
                                          + _dot(mixed, wout_ref[...]))

    abuf[:, 0:HALO, :] = abuf[:, rows:ext_rows, :]
    bbuf[:, 0:HALO, :] = bbuf[:, rows:ext_rows, :]


def _even_mixer(x3d, g, w_in, conv_w, conv_b, ln_g, ln_b, pool_w, pool_scale, w_out, *, layer):
    bsz, s, d = x3d.shape
    taps, a_width = conv_w.shape
    b_width = pool_scale.shape[1]
    n_groups, group_dim, _ = pool_w.shape
    lane_tiles = a_width // LANES
    conv_wb = jnp.broadcast_to(conv_w.reshape(taps, lane_tiles, 1, LANES),
                               (taps, lane_tiles, BF16_ROWS, LANES)).astype(BF16)
    pw = pool_w.reshape(n_groups // 2, 2, group_dim, group_dim)
    zero = jnp.zeros_like(pw[:, 0])
    pool_wb = jnp.concatenate(
        [jnp.concatenate([pw[:, 0], zero], axis=-1), jnp.concatenate([zero, pw[:, 1]], axis=-1)],
        axis=-2).astype(BF16)
    row_spec = pl.BlockSpec((None, MIX_ROWS, d), lambda b, t: (b, t, 0))
    ext_rows = HALO + MIX_ROWS
    return pl.pallas_call(
        functools.partial(_even_kernel, layer=layer),
        grid=(bsz, s // MIX_ROWS),
        in_specs=[row_spec, _resident(g.shape), _HBM, _resident(conv_wb.shape),
                  _resident(conv_b.shape), _resident(ln_g.shape), _resident(ln_b.shape),
                  _resident(pool_wb.shape), _resident(pool_scale.shape), _HBM],
        out_specs=row_spec,
        out_shape=jax.ShapeDtypeStruct((bsz, s, d), F32),
        scratch_shapes=_stage_scratch(*w_in.shape[1:]) + _stage_scratch(*w_out.shape[1:]) + [
            pltpu.VMEM((lane_tiles, ext_rows + SUBLANES, LANES), F32),
            pltpu.VMEM((lane_tiles, ext_rows // 2, LANES), U32),
            pltpu.VMEM((lane_tiles, ext_rows // 2, LANES), U32),
            pltpu.VMEM((n_groups, ext_rows, group_dim), F32),
            pltpu.VMEM((lane_tiles, MIX_ROWS, LANES), F32),
            pltpu.VMEM((MIX_ROWS, b_width), BF16)],
        compiler_params=pltpu.CompilerParams(
            dimension_semantics=("arbitrary", "arbitrary"), vmem_limit_bytes=VMEM_LIMIT_BYTES),
        name="even_mixer",
    )(x3d, g, w_in, conv_wb, conv_b, ln_g, ln_b, pool_wb, pool_scale, w_out)


def _gelu_tanh(x):
    c = 0.7978845608028654
    return 0.5 * x * (1.0 + jnp.tanh(c * (x + 0.044715 * (x * x * x))))


def _odd_kernel(x_ref, g_ref, win_hbm, lg_ref, lb_ref, sw_ref, sb_ref, cw_ref, wout_hbm,
                o_ref, win_ref, win_stage, win_sem, wout_ref, wout_stage, wout_sem, dbuf,
                *, layer):
    @pl.when(_first_grid_step())
    def _():
        _load_weights(layer, (win_hbm, win_ref, win_stage, win_sem),
                      (wout_hbm, wout_ref, wout_stage, wout_sem))

    rows = x_ref.shape[0]
    c_width = lg_ref.shape[1]
    d_width = cw_ref.shape[1]
    taps = cw_ref.shape[0]
    n_groups = sw_ref.shape[0]
    group_dim = c_width // n_groups
    lane_tiles = d_width // LANES

    @pl.when(pl.program_id(1) == 0)
    def _():
        dbuf[:, 0:HALO, :] = jnp.zeros((lane_tiles, HALO, LANES), F32)

    tri = (lax.broadcasted_iota(jnp.int32, (SGU_CHUNK, SGU_CHUNK), 0)
           >= lax.broadcasted_iota(jnp.int32, (SGU_CHUNK, SGU_CHUNK), 1))
    w_s = [jnp.where(tri, sw_ref[gi], 0.0).astype(BF16) for gi in range(n_groups)]

    for r0 in range(0, rows, MIX_SUB_ROWS):
        x = x_ref[r0:r0 + MIX_SUB_ROWS, :]
        h = _rmsnorm(x, g_ref[...]).astype(BF16)
        z = _dot(h, win_ref[...])
        c_u = _gelu_tanh(z[:, :c_width])
        c_v = _layernorm(_gelu_tanh(z[:, c_width:2 * c_width]), lg_ref[...],
                         lb_ref[...]).astype(BF16)
        off = 2 * c_width
        d_b = z[:, off:off + d_width]
        d_cx = z[:, off + d_width:off + 2 * d_width] * z[:, off + 2 * d_width:]
        for j in range(lane_tiles):
            dbuf[j, HALO + r0:HALO + r0 + MIX_SUB_ROWS, :] = d_cx[:, j * LANES:(j + 1) * LANES]

        chunk_rows = []
        for n in range(MIX_SUB_ROWS // SGU_CHUNK):
            parts = []
            for gi in range(n_groups):
                v = c_v[n * SGU_CHUNK:(n + 1) * SGU_CHUNK, gi * group_dim:(gi + 1) * group_dim]
                parts.append(_dot(w_s[gi], v) + sb_ref[:, gi:gi + 1])
            chunk_rows.append(jnp.concatenate(parts, axis=-1))
        c_out = c_u * jnp.concatenate(chunk_rows, axis=0)

        conv_tiles = []
        for j in range(lane_tiles):
            conv = jnp.zeros((MIX_SUB_ROWS, LANES), F32)
            for k in range(taps):
                start = HALO + r0 - (taps - 1) + k
                conv = conv + (cw_ref[k:k + 1, j * LANES:(j + 1) * LANES]
                               * dbuf[j, start:start + MIX_SUB_ROWS, :])
            conv_tiles.append(conv)
        d_out = d_b * jnp.concatenate(conv_tiles, axis=-1)

        mixed = jnp.concatenate([c_out, d_out], axis=-1).astype(BF16)
        o_ref[r0:r0 + MIX_SUB_ROWS, :] = x + _dot(mixed, wout_ref[...])

    dbuf[:, 0:HALO, :] = dbuf[:, rows:rows + HALO, :]


def _odd_mixer(x3d, g, w_in, ln_g, ln_b, sgu_w, sgu_b_t, conv_w, w_out, *, layer):
    bsz, s, d = x3d.shape
    d_width = conv_w.shape[1]
    row_spec = pl.BlockSpec((None, MIX_ROWS, d), lambda b, t: (b, t, 0))
    return pl.pallas_call(
        functools.partial(_odd_kernel, layer=layer),
        grid=(bsz, s // MIX_ROWS),
        in_specs=[row_spec, _resident(g.shape), _HBM, _resident(ln_g.shape),
                  _resident(ln_b.shape), _resident(sgu_w.shape), _resident(sgu_b_t.shape),
                  _resident(conv_w.shape), _HBM],
        out_specs=row_spec,
        out_shape=jax.ShapeDtypeStruct((bsz, s, d), F32),
        scratch_shapes=_stage_scratch(*w_in.shape[1:]) + _stage_scratch(*w_out.shape[1:]) + [
            pltpu.VMEM((d_width // LANES, HALO + MIX_ROWS, LANES), F32)],
        compiler_params=pltpu.CompilerParams(
            dimension_semantics=("arbitrary", "arbitrary"), vmem_limit_bytes=VMEM_LIMIT_BYTES),
        name="odd_mixer",
    )(x3d, g, w_in, ln_g, ln_b, sgu_w, sgu_b_t, conv_w, w_out)


def kernel(x, ffn1_norm, ffn1_w_gate, ffn1_w_up, ffn1_w_down, mix_norm, ffn2_norm, ffn2_w_gate, ffn2_w_up, ffn2_w_down, ev_w_in, ev_conv_w, ev_conv_b, ev_ln_g, ev_ln_b, ev_pool_w, ev_pool_scale, ev_w_out, od_w_in, od_sgu_ln_g, od_sgu_ln_b, od_sgu_w, od_sgu_b, od_conv_w, od_w_out, final_norm):
    bsz, s, d = x.shape
    depth = ffn1_norm.shape[0]
    assert s % MIX_ROWS == 0 and (bsz * s) % FFN_ROWS == 0
    assert FFN_ROWS % FFN_SUB_ROWS == 0 and MIX_ROWS % MIX_SUB_ROWS == 0
    assert MIX_SUB_ROWS % SGU_CHUNK == 0 and MIX_SUB_ROWS % CONV_ROWS == 0
    assert ev_conv_w.shape[1] - 1 <= HALO and max(POOL_WINDOWS) <= HALO

    def row(v):
        return v.reshape(1, -1)

    fn = row(final_norm)
    ffn1 = (ffn1_norm[:, None, :], ffn1_w_gate, ffn1_w_up, ffn1_w_down)
    ffn2 = (ffn2_norm[:, None, :], ffn2_w_gate, ffn2_w_up, ffn2_w_down)
    for i in range(depth):
        j = i // 2
        x2d = _ffn(x.reshape(bsz * s, d), *ffn1, fn, layer=i, final_norm=False)
        x = x2d.reshape(bsz, s, d)
        if i % 2 == 0:
            x = _even_mixer(x, row(mix_norm[i]), ev_w_in, ev_conv_w[j], row(ev_conv_b[j]),
                            row(ev_ln_g[j]), row(ev_ln_b[j]), ev_pool_w[j],
                            row(ev_pool_scale[j]), ev_w_out, layer=j)
        else:
            x = _odd_mixer(x, row(mix_norm[i]), od_w_in, row(od_sgu_ln_g[j]),
                           row(od_sgu_ln_b[j]), od_sgu_w[j], od_sgu_b[j].T, od_conv_w[j],
                           od_w_out, layer=j)
        x2d = _ffn(x.reshape(bsz * s, d), *ffn2, fn, layer=i, final_norm=(i == depth - 1))
        x = x2d.reshape(bsz, s, d)
    return x
```

```python
import functools

import jax
import jax.numpy as jnp
from jax import lax
from jax.experimental import pallas as pl
from jax.experimental.pallas import tpu as pltpu

EPS = 1e-6
FFN_RES_WEIGHT = 0.5
POOL_WINDOWS = (2, 4, 8, 16)
SGU_CHUNK = 128
LANES = 128
SUBLANES = 8
BF16_ROWS = 2 * SUBLANES
HALO = 32
VMEM_LIMIT_BYTES = 56 * 1024 * 1024

FFN_ROWS = 1024
FFN_SUB_ROWS = 128
EVEN_ROWS = 1024
ODD_ROWS = 2048
MIX_SUB_ROWS = 256
CONV_ROWS = 256

BF16 = jnp.bfloat16
F32 = jnp.float32
U32 = jnp.uint32


def _rmsnorm(x, g):
    return x * lax.rsqrt(jnp.mean(x * x, axis=-1, keepdims=True) + EPS) * g


def _layernorm(x, g, b):
    mu = jnp.mean(x, axis=-1, keepdims=True)
    xc = x - mu
    var = jnp.mean(xc * xc, axis=-1, keepdims=True)
    return xc * lax.rsqrt(var + EPS) * g + b


def _dot(a, b):
    return jnp.dot(a, b, preferred_element_type=F32)


def _resident(shape):
    zeros = (0,) * len(shape)
    return pl.BlockSpec(shape, lambda *_: zeros, pipeline_mode=pl.Buffered(1))


def _layer_resident(shape, layer):
    index = (layer,) + (0,) * (len(shape) - 1)
    return pl.BlockSpec((None,) + tuple(shape[1:]), lambda *_: index,
                        pipeline_mode=pl.Buffered(1))


_HBM = pl.BlockSpec(memory_space=pl.ANY)


STAGE_CHUNKS = 8


def _stage_scratch(k, n):
    rows = k // STAGE_CHUNKS
    assert rows * STAGE_CHUNKS == k and rows % SUBLANES == 0
    return [pltpu.VMEM((k, n), BF16), pltpu.VMEM((2, rows, n), F32),
            pltpu.SemaphoreType.DMA((2,))]


def _load_weights(layer, *weights):
    def chunk_copy(w, c):
        w_hbm, _, stage, sem = w
        rows = stage.shape[1]
        return pltpu.make_async_copy(w_hbm.at[layer, pl.ds(c * rows, rows), :],
                                     stage.at[c % 2], sem.at[c % 2])

    for w in weights:
        chunk_copy(w, 0).start()
    for c in range(STAGE_CHUNKS):
        for w in weights:
            _, w_vmem, stage, _ = w
            rows = stage.shape[1]
            if c + 1 < STAGE_CHUNKS:
                chunk_copy(w, c + 1).start()
            chunk_copy(w, c).wait()
            w_vmem[c * rows:(c + 1) * rows, :] = stage[c % 2].astype(BF16)


def _ffn_kernel(x_ref, g_ref, wg_hbm, wu_hbm, wd_hbm, fn_ref, o_ref,
                wg_ref, wg_stage, wg_sem, wu_ref, wu_stage, wu_sem, wd_ref, wd_stage, wd_sem,
                *, layer, final_norm):
    @pl.when(pl.program_id(0) == 0)
    def _():
        _load_weights(layer, (wg_hbm, wg_ref, wg_stage, wg_sem),
                      (wu_hbm, wu_ref, wu_stage, wu_sem), (wd_hbm, wd_ref, wd_stage, wd_sem))

    for r0 in range(0, x_ref.shape[0], FFN_SUB_ROWS):
        rows = slice(r0, r0 + FFN_SUB_ROWS)
        x = x_ref[rows, :]
        h = _rmsnorm(x, g_ref[...]).astype(BF16)
        gate = _dot(h, wg_ref[...])
        up = _dot(h, wu_ref[...])
        act = (gate * jax.nn.sigmoid(gate) * up).astype(BF16)
        out = x + FFN_RES_WEIGHT * _dot(act, wd_ref[...])
        if final_norm:
            out = _rmsnorm(out, fn_ref[...])
        o_ref[rows, :] = out


def _ffn(x2d, g, wg, wu, wd, fn, *, layer, final_norm):
    n, d = x2d.shape
    dff = wg.shape[2]
    row_spec = pl.BlockSpec((FFN_ROWS, d), lambda i: (i, 0))
    return pl.pallas_call(
        functools.partial(_ffn_kernel, layer=layer, final_norm=final_norm),
        grid=(n // FFN_ROWS,),
        in_specs=[row_spec, _layer_resident(g.shape, layer), _HBM, _HBM, _HBM,
                  _resident(fn.shape)],
        out_specs=row_spec,
        out_shape=jax.ShapeDtypeStruct((n, d), F32),
        scratch_shapes=_stage_scratch(d, dff) + _stage_scratch(d, dff) + _stage_scratch(dff, d),
        compiler_params=pltpu.CompilerParams(
            dimension_semantics=("arbitrary",), vmem_limit_bytes=VMEM_LIMIT_BYTES),
        name="ffn_final" if final_norm else "ffn",
    )(x2d, g, wg, wu, wd, fn)


def _first_grid_step():
    return jnp.logical_and(pl.program_id(0) == 0, pl.program_id(1) == 0)


def _even_kernel(x_ref, g_ref, win_hbm, cw_ref, cb_ref, lg_ref, lb_ref, pw_ref, ps_ref,
                 wout_hbm, o_ref, win_ref, win_stage, win_sem, wout_ref, wout_stage, wout_sem,
                 abuf, pk_even, pk_odd, bbuf, cbuf, dbuf, *, layer):
    @pl.when(_first_grid_step())
    def _():
        _load_weights(layer, (win_hbm, win_ref, win_stage, win_sem),
                      (wout_hbm, wout_ref, wout_stage, wout_sem))

    rows = x_ref.shape[0]
    taps, lane_tiles, pack_rows, _ = cw_ref.shape
    a_width = lane_tiles * LANES
    n_groups, _, group_dim = bbuf.shape
    ext_rows = HALO + rows

    @pl.when(pl.program_id(1) == 0)
    def _():
        abuf[:, 0:HALO, :] = jnp.zeros((lane_tiles, HALO, LANES), F32)
        abuf[:, ext_rows:ext_rows + SUBLANES, :] = jnp.zeros((lane_tiles, SUBLANES, LANES), F32)
        bbuf[:, 0:HALO, :] = jnp.zeros((n_groups, HALO, group_dim), F32)

    for r0 in range(0, rows, MIX_SUB_ROWS):
        h = _rmsnorm(x_ref[r0:r0 + MIX_SUB_ROWS, :], g_ref[...]).astype(BF16)
        z = _dot(h, win_ref[...])
        a_val = z[:, :a_width]
        a_gate = z[:, a_width:2 * a_width]
        glu = a_val * jax.nn.sigmoid(a_gate)
        for j in range(lane_tiles):
            abuf[j, HALO + r0:HALO + r0 + MIX_SUB_ROWS, :] = glu[:, j * LANES:(j + 1) * LANES]
        for gi in range(n_groups):
            c0 = 2 * a_width + gi * group_dim
            bbuf[gi, HALO + r0:HALO + r0 + MIX_SUB_ROWS, :] = z[:, c0:c0 + group_dim]

    for j in range(lane_tiles):
        pk_even[j] = pltpu.bitcast(abuf[j, 0:ext_rows, :].astype(BF16), U32)
        pk_odd[j] = pltpu.bitcast(abuf[j, 1:ext_rows + 1, :].astype(BF16), U32)
    first = HALO - (taps - 1)
    words = pack_rows // 2

    def conv_lane_tile(j, carry):
        for blk in range(rows // CONV_ROWS):
            accs = [jnp.zeros((pack_rows, LANES), F32) for _ in range(CONV_ROWS // pack_rows)]
            for k in range(taps):
                off = blk * CONV_ROWS + first + k
                src = pk_odd if off % 2 else pk_even
                wk = cw_ref[k, j].astype(F32)
                for gi in range(len(accs)):
                    w0 = off // 2 + gi * words
                    xk = pltpu.bitcast(src[j, w0:w0 + words, :], BF16)
                    accs[gi] = accs[gi] + xk.astype(F32) * wk
            for gi, acc in enumerate(accs):
                q0 = blk * CONV_ROWS + gi * pack_rows
                cbuf[j, q0:q0 + pack_rows, :] = acc
        return carry

    lax.fori_loop(0, lane_tiles, conv_lane_tile, 0)

    pos = lax.broadcasted_iota(jnp.int32, (CONV_ROWS, LANES), 0) + pl.program_id(1) * rows + 1
    pair = 2 * group_dim
    for r0 in range(0, rows, MIX_SUB_ROWS):
        conv = jnp.concatenate([cbuf[j, r0:r0 + MIX_SUB_ROWS, :] for j in range(lane_tiles)],
                               axis=-1)
        a = _layernorm(conv + cb_ref[...], lg_ref[...], lb_ref[...])
        a = a * jax.nn.sigmoid(a)
        for gi, win in enumerate(POOL_WINDOWS):
            lanes = slice(gi * group_dim, (gi + 1) * group_dim)
            for q0 in range(r0, r0 + MIX_SUB_ROWS, CONV_ROWS):
                v = bbuf[gi, HALO + q0:HALO + q0 + CONV_ROWS, :]
                wsum = v
                for back in range(1, win):
                    wsum = wsum + bbuf[gi, HALO + q0 - back:HALO + q0 - back + CONV_ROWS, :]
                if q0 == 0:
                    mean = wsum / jnp.minimum(pos, win).astype(F32)
                else:
                    mean = wsum * (1.0 / win)
                dbuf[q0:q0 + CONV_ROWS, lanes] = (mean - v).astype(BF16)
        b_parts = [_dot(dbuf[r0:r0 + MIX_SUB_ROWS, p * pair:(p + 1) * pair], pw_ref[p])
                   for p in range(n_groups // 2)]
        b = jnp.concatenate(b_parts, axis=-1) * ps_ref[...]
        mixed = jnp.concatenate([a, b], axis=-1).astype(BF16)
        o_ref[r0:r0 + MIX_SUB_ROWS, :] = (x_ref[r0:r0 + MIX_SUB_ROWS, :]
                                          + _dot(mixed, wout_ref[...]))

    abuf[:, 0:HALO, :] = abuf[:, rows:ext_rows, :]
    bbuf[:, 0:HALO, :] = bbuf[:, rows:ext_rows, :]


def _even_mixer(x3d, g, w_in, conv_w, conv_b, ln_g, ln_b, pool_w, pool_scale, w_out, *, layer):
    bsz, s, d = x3d.shape
    taps, a_width = conv_w.shape
    b_width = pool_scale.shape[1]
    n_groups, group_dim, _ = pool_w.shape
    lane_tiles = a_width // LANES
    conv_wb = jnp.broadcast_to(conv_w.reshape(taps, lane_tiles, 1, LANES),
                               (taps, lane_tiles, BF16_ROWS, LANES)).astype(BF16)
    pw = pool_w.reshape(n_groups // 2, 2, group_dim, group_dim)
    zero = jnp.zeros_like(pw[:, 0])
    pool_wb = jnp.concatenate(
        [jnp.concatenate([pw[:, 0], zero], axis=-1), jnp.concatenate([zero, pw[:, 1]], axis=-1)],
        axis=-2).astype(BF16)
    row_spec = pl.BlockSpec((None, EVEN_ROWS, d), lambda b, t: (b, t, 0))
    ext_rows = HALO + EVEN_ROWS
    return pl.pallas_call(
        functools.partial(_even_kernel, layer=layer),
        grid=(bsz, s // EVEN_ROWS),
        in_specs=[row_spec, _resident(g.shape), _HBM, _resident(conv_wb.shape),
                  _resident(conv_b.shape), _resident(ln_g.shape), _resident(ln_b.shape),
                  _resident(pool_wb.shape), _resident(pool_scale.shape), _HBM],
        out_specs=row_spec,
        out_shape=jax.ShapeDtypeStruct((bsz, s, d), F32),
        scratch_shapes=_stage_scratch(*w_in.shape[1:]) + _stage_scratch(*w_out.shape[1:]) + [
            pltpu.VMEM((lane_tiles, ext_rows + SUBLANES, LANES), F32),
            pltpu.VMEM((lane_tiles, ext_rows // 2, LANES), U32),
            pltpu.VMEM((lane_tiles, ext_rows // 2, LANES), U32),
            pltpu.VMEM((n_groups, ext_rows, group_dim), F32),
            pltpu.VMEM((lane_tiles, EVEN_ROWS, LANES), F32),
            pltpu.VMEM((EVEN_ROWS, b_width), BF16)],
        compiler_params=pltpu.CompilerParams(
            dimension_semantics=("arbitrary", "arbitrary"), vmem_limit_bytes=VMEM_LIMIT_BYTES),
        name="even_mixer",
    )(x3d, g, w_in, conv_wb, conv_b, ln_g, ln_b, pool_wb, pool_scale, w_out)


def _gelu_tanh(x):
    c = 0.7978845608028654
    return 0.5 * x * (1.0 + jnp.tanh(c * (x + 0.044715 * (x * x * x))))


def _odd_kernel(x_ref, g_ref, win_hbm, lg_ref, lb_ref, sw_ref, sb_ref, cw_ref, wout_hbm,
                o_ref, win_ref, win_stage, win_sem, wout_ref, wout_stage, wout_sem, dbuf,
                *, layer):
    @pl.when(_first_grid_step())
    def _():
        _load_weights(layer, (win_hbm, win_ref, win_stage, win_sem),
                      (wout_hbm, wout_ref, wout_stage, wout_sem))

    rows = x_ref.shape[0]
    c_width = lg_ref.shape[1]
    d_width = cw_ref.shape[1]
    taps = cw_ref.shape[0]
    n_groups = sw_ref.shape[0]
    group_dim = c_width // n_groups
    lane_tiles = d_width // LANES

    @pl.when(pl.program_id(1) == 0)
    def _():
        dbuf[:, 0:HALO, :] = jnp.zeros((lane_tiles, HALO, LANES), F32)

    tri = (lax.broadcasted_iota(jnp.int32, (SGU_CHUNK, SGU_CHUNK), 0)
           >= lax.broadcasted_iota(jnp.int32, (SGU_CHUNK, SGU_CHUNK), 1))
    w_s = [jnp.where(tri, sw_ref[gi], 0.0).astype(BF16) for gi in range(n_groups)]

    for r0 in range(0, rows, MIX_SUB_ROWS):
        x = x_ref[r0:r0 + MIX_SUB_ROWS, :]
        h = _rmsnorm(x, g_ref[...]).astype(BF16)
        z = _dot(h, win_ref[...])
        c_u = _gelu_tanh(z[:, :c_width])
        c_v = _layernorm(_gelu_tanh(z[:, c_width:2 * c_width]), lg_ref[...],
                         lb_ref[...]).astype(BF16)
        off = 2 * c_width
        d_b = z[:, off:off + d_width]
        d_cx = z[:, off + d_width:off + 2 * d_width] * z[:, off + 2 * d_width:]
        for j in range(lane_tiles):
            dbuf[j, HALO + r0:HALO + r0 + MIX_SUB_ROWS, :] = d_cx[:, j * LANES:(j + 1) * LANES]

        chunk_rows = []
        for n in range(MIX_SUB_ROWS // SGU_CHUNK):
            parts = []
            for gi in range(n_groups):
                v = c_v[n * SGU_CHUNK:(n + 1) * SGU_CHUNK, gi * group_dim:(gi + 1) * group_dim]
                parts.append(_dot(w_s[gi], v) + sb_ref[:, gi:gi + 1])
            chunk_rows.append(jnp.concatenate(parts, axis=-1))
        c_out = c_u * jnp.concatenate(chunk_rows, axis=0)

        conv_tiles = []
        for j in range(lane_tiles):
            conv = jnp.zeros((MIX_SUB_ROWS, LANES), F32)
            for k in range(taps):
                start = HALO + r0 - (taps - 1) + k
                conv = conv + (cw_ref[k:k + 1, j * LANES:(j + 1) * LANES]
                               * dbuf[j, start:start + MIX_SUB_ROWS, :])
            conv_tiles.append(conv)
        d_out = d_b * jnp.concatenate(conv_tiles, axis=-1)

        mixed = jnp.concatenate([c_out, d_out], axis=-1).astype(BF16)
        o_ref[r0:r0 + MIX_SUB_ROWS, :] = x + _dot(mixed, wout_ref[...])

    dbuf[:, 0:HALO, :] = dbuf[:, rows:rows + HALO, :]


def _odd_mixer(x3d, g, w_in, ln_g, ln_b, sgu_w, sgu_b_t, conv_w, w_out, *, layer):
    bsz, s, d = x3d.shape
    d_width = conv_w.shape[1]
    row_spec = pl.BlockSpec((None, ODD_ROWS, d), lambda b, t: (b, t, 0))
    return pl.pallas_call(
        functools.partial(_odd_kernel, layer=layer),
        grid=(bsz, s // ODD_ROWS),
        in_specs=[row_spec, _resident(g.shape), _HBM, _resident(ln_g.shape),
                  _resident(ln_b.shape), _resident(sgu_w.shape), _resident(sgu_b_t.shape),
                  _resident(conv_w.shape), _HBM],
        out_specs=row_spec,
        out_shape=jax.ShapeDtypeStruct((bsz, s, d), F32),
        scratch_shapes=_stage_scratch(*w_in.shape[1:]) + _stage_scratch(*w_out.shape[1:]) + [
            pltpu.VMEM((d_width // LANES, HALO + ODD_ROWS, LANES), F32)],
        compiler_params=pltpu.CompilerParams(
            dimension_semantics=("arbitrary", "arbitrary"), vmem_limit_bytes=VMEM_LIMIT_BYTES),
        name="odd_mixer",
    )(x3d, g, w_in, ln_g, ln_b, sgu_w, sgu_b_t, conv_w, w_out)


def kernel(x, ffn1_norm, ffn1_w_gate, ffn1_w_up, ffn1_w_down, mix_norm, ffn2_norm, ffn2_w_gate, ffn2_w_up, ffn2_w_down, ev_w_in, ev_conv_w, ev_conv_b, ev_ln_g, ev_ln_b, ev_pool_w, ev_pool_scale, ev_w_out, od_w_in, od_sgu_ln_g, od_sgu_ln_b, od_sgu_w, od_sgu_b, od_conv_w, od_w_out, final_norm):
    bsz, s, d = x.shape
    depth = ffn1_norm.shape[0]
    assert s % EVEN_ROWS == 0 and s % ODD_ROWS == 0 and (bsz * s) % FFN_ROWS == 0
    assert FFN_ROWS % FFN_SUB_ROWS == 0
    assert EVEN_ROWS % MIX_SUB_ROWS == 0 and ODD_ROWS % MIX_SUB_ROWS == 0
    assert MIX_SUB_ROWS % SGU_CHUNK == 0 and MIX_SUB_ROWS % CONV_ROWS == 0
    assert ev_conv_w.shape[1] - 1 <= HALO and max(POOL_WINDOWS) <= HALO

    def row(v):
        return v.reshape(1, -1)

    fn = row(final_norm)
    ffn1 = (ffn1_norm[:, None, :], ffn1_w_gate, ffn1_w_up, ffn1_w_down)
    ffn2 = (ffn2_norm[:, None, :], ffn2_w_gate, ffn2_w_up, ffn2_w_down)
    for i in range(depth):
        j = i // 2
        x2d = _ffn(x.reshape(bsz * s, d), *ffn1, fn, layer=i, final_norm=False)
        x = x2d.reshape(bsz, s, d)
        if i % 2 == 0:
            x = _even_mixer(x, row(mix_norm[i]), ev_w_in, ev_conv_w[j], row(ev_conv_b[j]),
                            row(ev_ln_g[j]), row(ev_ln_b[j]), ev_pool_w[j],
                            row(ev_pool_scale[j]), ev_w_out, layer=j)
        else:
            x = _odd_mixer(x, row(mix_norm[i]), od_w_in, row(od_sgu_ln_g[j]),
                           row(od_sgu_ln_b[j]), od_sgu_w[j], od_sgu_b[j].T, od_conv_w[j],
                           od_w_out, layer=j)
        x2d = _ffn(x.reshape(bsz * s, d), *ffn2, fn, layer=i, final_norm=(i == depth - 1))
        x = x2d.reshape(bsz, s, d)
    return x
```

```python
import functools

import jax
import jax.numpy as jnp
from jax import lax
from jax.experimental import pallas as pl
from jax.experimental.pallas import tpu as pltpu

EPS = 1e-6
FFN_RES_WEIGHT = 0.5
POOL_WINDOWS = (2, 4, 8, 16)
SGU_CHUNK = 128
LANES = 128
SUBLANES = 8
BF16_ROWS = 2 * SUBLANES
HALO = 32
VMEM_LIMIT_BYTES = 56 * 1024 * 1024

FFN_ROWS = 1024
FFN_SUB_ROWS = 128
EVEN_ROWS = 1024
ODD_ROWS = 2048
MIX_SUB_ROWS = 256
CONV_ROWS = 256

BF16 = jnp.bfloat16
F32 = jnp.float32
U32 = jnp.uint32


def _rmsnorm(x, g):
    return x * lax.rsqrt(jnp.mean(x * x, axis=-1, keepdims=True) + EPS) * g


def _layernorm(x, g, b):
    mu = jnp.mean(x, axis=-1, keepdims=True)
    xc = x - mu
    var = jnp.mean(xc * xc, axis=-1, keepdims=True)
    return xc * lax.rsqrt(var + EPS) * g + b


def _dot(a, b):
    return jnp.dot(a, b, preferred_element_type=F32)


def _resident(shape):
    zeros = (0,) * len(shape)
    return pl.BlockSpec(shape, lambda *_: zeros, pipeline_mode=pl.Buffered(1))


def _layer_resident(shape, layer):
    index = (layer,) + (0,) * (len(shape) - 1)
    return pl.BlockSpec((None,) + tuple(shape[1:]), lambda *_: index,
                        pipeline_mode=pl.Buffered(1))


_HBM = pl.BlockSpec(memory_space=pl.ANY)


STAGE_CHUNKS = 8


def _stage_scratch(k, n):
    rows = k // STAGE_CHUNKS
    assert rows * STAGE_CHUNKS == k and rows % SUBLANES == 0
    return [pltpu.VMEM((k, n), BF16), pltpu.VMEM((2, rows, n), F32),
            pltpu.SemaphoreType.DMA((2,))]


def _load_weights(layer, *weights):
    def chunk_copy(w, c):
        w_hbm, _, stage, sem = w
        rows = stage.shape[1]
        return pltpu.make_async_copy(w_hbm.at[layer, pl.ds(c * rows, rows), :],
                                     stage.at[c % 2], sem.at[c % 2])

    for w in weights:
        chunk_copy(w, 0).start()
    for c in range(STAGE_CHUNKS):
        for w in weights:
            _, w_vmem, stage, _ = w
            rows = stage.shape[1]
            if c + 1 < STAGE_CHUNKS:
                chunk_copy(w, c + 1).start()
            chunk_copy(w, c).wait()
            w_vmem[c * rows:(c + 1) * rows, :] = stage[c % 2].astype(BF16)


def _ffn_kernel(x_ref, g_ref, wg_hbm, wu_hbm, wd_hbm, fn_ref, o_ref,
                wg_ref, wg_stage, wg_sem, wu_ref, wu_stage, wu_sem, wd_ref, wd_stage, wd_sem,
                *, layer, final_norm):
    @pl.when(pl.program_id(0) == 0)
    def _():
        _load_weights(layer, (wg_hbm, wg_ref, wg_stage, wg_sem),
                      (wu_hbm, wu_ref, wu_stage, wu_sem), (wd_hbm, wd_ref, wd_stage, wd_sem))

    for r0 in range(0, x_ref.shape[0], FFN_SUB_ROWS):
        rows = slice(r0, r0 + FFN_SUB_ROWS)
        x = x_ref[rows, :]
        h = _rmsnorm(x, g_ref[...]).astype(BF16)
        gate = _dot(h, wg_ref[...])
        up = _dot(h, wu_ref[...])
        act = (gate * jax.nn.sigmoid(gate) * up).astype(BF16)
        out = x + FFN_RES_WEIGHT * _dot(act, wd_ref[...])
        if final_norm:
            out = _rmsnorm(out, fn_ref[...])
        o_ref[rows, :] = out


def _ffn(x2d, g, wg, wu, wd, fn, *, layer, final_norm, in_place):
    n, d = x2d.shape
    dff = wg.shape[2]
    row_spec = pl.BlockSpec((FFN_ROWS, d), lambda i: (i, 0))
    return pl.pallas_call(
        functools.partial(_ffn_kernel, layer=layer, final_norm=final_norm),
        grid=(n // FFN_ROWS,),
        in_specs=[row_spec, _layer_resident(g.shape, layer), _HBM, _HBM, _HBM,
                  _resident(fn.shape)],
        out_specs=row_spec,
        out_shape=jax.ShapeDtypeStruct((n, d), F32),
        scratch_shapes=_stage_scratch(d, dff) + _stage_scratch(d, dff) + _stage_scratch(dff, d),
        input_output_aliases={0: 0} if in_place else {},
        compiler_params=pltpu.CompilerParams(
            dimension_semantics=("arbitrary",), vmem_limit_bytes=VMEM_LIMIT_BYTES),
        name="ffn_final" if final_norm else "ffn",
    )(x2d, g, wg, wu, wd, fn)


def _first_grid_step():
    return jnp.logical_and(pl.program_id(0) == 0, pl.program_id(1) == 0)


def _even_kernel(x_ref, g_ref, win_hbm, cw_ref, cb_ref, lg_ref, lb_ref, pw_ref, ps_ref,
                 wout_hbm, o_ref, win_ref, win_stage, win_sem, wout_ref, wout_stage, wout_sem,
                 abuf, pk_even, pk_odd, bbuf, cbuf, dbuf, *, layer):
    @pl.when(_first_grid_step())
    def _():
        _load_weights(layer, (win_hbm, win_ref, win_stage, win_sem),
                      (wout_hbm, wout_ref, wout_stage, wout_sem))

    rows = x_ref.shape[0]
    taps, lane_tiles, pack_rows, _ = cw_ref.shape
    a_width = lane_tiles * LANES
    n_groups, _, group_dim = bbuf.shape
    ext_rows = HALO + rows

    @pl.when(pl.program_id(1) == 0)
    def _():
        abuf[:, 0:HALO, :] = jnp.zeros((lane_tiles, HALO, LANES), F32)
        abuf[:, ext_rows:ext_rows + SUBLANES, :] = jnp.zeros((lane_tiles, SUBLANES, LANES), F32)
        bbuf[:, 0:HALO, :] = jnp.zeros((n_groups, HALO, group_dim), F32)

    for r0 in range(0, rows, MIX_SUB_ROWS):
        h = _rmsnorm(x_ref[r0:r0 + MIX_SUB_ROWS, :], g_ref[...]).astype(BF16)
        z = _dot(h, win_ref[...])
        a_val = z[:, :a_width]
        a_gate = z[:, a_width:2 * a_width]
        glu = a_val * jax.nn.sigmoid(a_gate)
        for j in range(lane_tiles):
            abuf[j, HALO + r0:HALO + r0 + MIX_SUB_ROWS, :] = glu[:, j * LANES:(j + 1) * LANES]
        for gi in range(n_groups):
            c0 = 2 * a_width + gi * group_dim
            bbuf[gi, HALO + r0:HALO + r0 + MIX_SUB_ROWS, :] = z[:, c0:c0 + group_dim]

    for j in range(lane_tiles):
        pk_even[j] = pltpu.bitcast(abuf[j, 0:ext_rows, :].astype(BF16), U32)
        pk_odd[j] = pltpu.bitcast(abuf[j, 1:ext_rows + 1, :].astype(BF16), U32)
    first = HALO - (taps - 1)
    words = pack_rows // 2

    def conv_lane_tile(j, carry):
        for blk in range(rows // CONV_ROWS):
            accs = [jnp.zeros((pack_rows, LANES), F32) for _ in range(CONV_ROWS // pack_rows)]
            for k in range(taps):
                off = blk * CONV_ROWS + first + k
                src = pk_odd if off % 2 else pk_even
                wk = cw_ref[k, j].astype(F32)
                for gi in range(len(accs)):
                    w0 = off // 2 + gi * words
                    xk = pltpu.bitcast(src[j, w0:w0 + words, :], BF16)
                    accs[gi] = accs[gi] + xk.astype(F32) * wk
            for gi, acc in enumerate(accs):
                q0 = blk * CONV_ROWS + gi * pack_rows
                cbuf[j, q0:q0 + pack_rows, :] = acc
        return carry

    lax.fori_loop(0, lane_tiles, conv_lane_tile, 0)

    pos = lax.broadcasted_iota(jnp.int32, (CONV_ROWS, LANES), 0) + pl.program_id(1) * rows + 1
    pair = 2 * group_dim
    for r0 in range(0, rows, MIX_SUB_ROWS):
        conv = jnp.concatenate([cbuf[j, r0:r0 + MIX_SUB_ROWS, :] for j in range(lane_tiles)],
                               axis=-1)
        a = _layernorm(conv + cb_ref[...], lg_ref[...], lb_ref[...])
        a = a * jax.nn.sigmoid(a)
        for gi, win in enumerate(POOL_WINDOWS):
            lanes = slice(gi * group_dim, (gi + 1) * group_dim)
            for q0 in range(r0, r0 + MIX_SUB_ROWS, CONV_ROWS):
                v = bbuf[gi, HALO + q0:HALO + q0 + CONV_ROWS, :]
                wsum = v
                for back in range(1, win):
                    wsum = wsum + bbuf[gi, HALO + q0 - back:HALO + q0 - back + CONV_ROWS, :]
                if q0 == 0:
                    mean = wsum / jnp.minimum(pos, win).astype(F32)
                else:
                    mean = wsum * (1.0 / win)
                dbuf[q0:q0 + CONV_ROWS, lanes] = (mean - v).astype(BF16)
        b_parts = [_dot(dbuf[r0:r0 + MIX_SUB_ROWS, p * pair:(p + 1) * pair], pw_ref[p])
                   for p in range(n_groups // 2)]
        b = jnp.concatenate(b_parts, axis=-1) * ps_ref[...]
        mixed = jnp.concatenate([a, b], axis=-1).astype(BF16)
        o_ref[r0:r0 + MIX_SUB_ROWS, :] = (x_ref[r0:r0 + MIX_SUB_ROWS, :]
                                          + _dot(mixed, wout_ref[...]))

    abuf[:, 0:HALO, :] = abuf[:, rows:ext_rows, :]
    bbuf[:, 0:HALO, :] = bbuf[:, rows:ext_rows, :]


def _even_mixer(x3d, g, w_in, conv_w, conv_b, ln_g, ln_b, pool_w, pool_scale, w_out, *, layer):
    bsz, s, d = x3d.shape
    taps, a_width = conv_w.shape
    b_width = pool_scale.shape[1]
    n_groups, group_dim, _ = pool_w.shape
    lane_tiles = a_width // LANES
    conv_wb = jnp.broadcast_to(conv_w.reshape(taps, lane_tiles, 1, LANES),
                               (taps, lane_tiles, BF16_ROWS, LANES)).astype(BF16)
    pw = pool_w.reshape(n_groups // 2, 2, group_dim, group_dim)
    zero = jnp.zeros_like(pw[:, 0])
    pool_wb = jnp.concatenate(
        [jnp.concatenate([pw[:, 0], zero], axis=-1), jnp.concatenate([zero, pw[:, 1]], axis=-1)],
        axis=-2).astype(BF16)
    row_spec = pl.BlockSpec((None, EVEN_ROWS, d), lambda b, t: (b, t, 0))
    ext_rows = HALO + EVEN_ROWS
    return pl.pallas_call(
        functools.partial(_even_kernel, layer=layer),
        grid=(bsz, s // EVEN_ROWS),
        in_specs=[row_spec, _resident(g.shape), _HBM, _resident(conv_wb.shape),
                  _resident(conv_b.shape), _resident(ln_g.shape), _resident(ln_b.shape),
                  _resident(pool_wb.shape), _resident(pool_scale.shape), _HBM],
        out_specs=row_spec,
        out_shape=jax.ShapeDtypeStruct((bsz, s, d), F32),
        scratch_shapes=_stage_scratch(*w_in.shape[1:]) + _stage_scratch(*w_out.shape[1:]) + [
            pltpu.VMEM((lane_tiles, ext_rows + SUBLANES, LANES), F32),
            pltpu.VMEM((lane_tiles, ext_rows // 2, LANES), U32),
            pltpu.VMEM((lane_tiles, ext_rows // 2, LANES), U32),
            pltpu.VMEM((n_groups, ext_rows, group_dim), F32),
            pltpu.VMEM((lane_tiles, EVEN_ROWS, LANES), F32),
            pltpu.VMEM((EVEN_ROWS, b_width), BF16)],
        input_output_aliases={0: 0},
        compiler_params=pltpu.CompilerParams(
            dimension_semantics=("arbitrary", "arbitrary"), vmem_limit_bytes=VMEM_LIMIT_BYTES),
        name="even_mixer",
    )(x3d, g, w_in, conv_wb, conv_b, ln_g, ln_b, pool_wb, pool_scale, w_out)


def _gelu_tanh(x):
    c = 0.7978845608028654
    return 0.5 * x * (1.0 + jnp.tanh(c * (x + 0.044715 * (x * x * x))))


def _odd_kernel(x_ref, g_ref, win_hbm, lg_ref, lb_ref, sw_ref, sb_ref, cw_ref, wout_hbm,
                o_ref, win_ref, win_stage, win_sem, wout_ref, wout_stage, wout_sem, dbuf,
                *, layer):
    @pl.when(_first_grid_step())
    def _():
        _load_weights(layer, (win_hbm, win_ref, win_stage, win_sem),
                      (wout_hbm, wout_ref, wout_stage, wout_sem))

    rows = x_ref.shape[0]
    c_width = lg_ref.shape[1]
    d_width = cw_ref.shape[1]
    taps = cw_ref.shape[0]
    n_groups = sw_ref.shape[0]
    group_dim = c_width // n_groups
    lane_tiles = d_width // LANES

    @pl.when(pl.program_id(1) == 0)
    def _():
        dbuf[:, 0:HALO, :] = jnp.zeros((lane_tiles, HALO, LANES), F32)

    tri = (lax.broadcasted_iota(jnp.int32, (SGU_CHUNK, SGU_CHUNK), 0)
           >= lax.broadcasted_iota(jnp.int32, (SGU_CHUNK, SGU_CHUNK), 1))
    w_s = [jnp.where(tri, sw_ref[gi], 0.0).astype(BF16) for gi in range(n_groups)]

    for r0 in range(0, rows, MIX_SUB_ROWS):
        x = x_ref[r0:r0 + MIX_SUB_ROWS, :]
        h = _rmsnorm(x, g_ref[...]).astype(BF16)
        z = _dot(h, win_ref[...])
        c_u = _gelu_tanh(z[:, :c_width])
        c_v = _layernorm(_gelu_tanh(z[:, c_width:2 * c_width]), lg_ref[...],
                         lb_ref[...]).astype(BF16)
        off = 2 * c_width
        d_b = z[:, off:off + d_width]
        d_cx = z[:, off + d_width:off + 2 * d_width] * z[:, off + 2 * d_width:]
        for j in range(lane_tiles):
            dbuf[j, HALO + r0:HALO + r0 + MIX_SUB_ROWS, :] = d_cx[:, j * LANES:(j + 1) * LANES]

        chunk_rows = []
        for n in range(MIX_SUB_ROWS // SGU_CHUNK):
            parts = []
            for gi in range(n_groups):
                v = c_v[n * SGU_CHUNK:(n + 1) * SGU_CHUNK, gi * group_dim:(gi + 1) * group_dim]
                parts.append(_dot(w_s[gi], v) + sb_ref[:, gi:gi + 1])
            chunk_rows.append(jnp.concatenate(parts, axis=-1))
        c_out = c_u * jnp.concatenate(chunk_rows, axis=0)

        conv_tiles = []
        for j in range(lane_tiles):
            conv = jnp.zeros((MIX_SUB_ROWS, LANES), F32)
            for k in range(taps):
                start = HALO + r0 - (taps - 1) + k
                conv = conv + (cw_ref[k:k + 1, j * LANES:(j + 1) * LANES]
                               * dbuf[j, start:start + MIX_SUB_ROWS, :])
            conv_tiles.append(conv)
        d_out = d_b * jnp.concatenate(conv_tiles, axis=-1)

        mixed = jnp.concatenate([c_out, d_out], axis=-1).astype(BF16)
        o_ref[r0:r0 + MIX_SUB_ROWS, :] = x + _dot(mixed, wout_ref[...])

    dbuf[:, 0:HALO, :] = dbuf[:, rows:rows + HALO, :]


def _odd_mixer(x3d, g, w_in, ln_g, ln_b, sgu_w, sgu_b_t, conv_w, w_out, *, layer):
    bsz, s, d = x3d.shape
    d_width = conv_w.shape[1]
    row_spec = pl.BlockSpec((None, ODD_ROWS, d), lambda b, t: (b, t, 0))
    return pl.pallas_call(
        functools.partial(_odd_kernel, layer=layer),
        grid=(bsz, s // ODD_ROWS),
        in_specs=[row_spec, _resident(g.shape), _HBM, _resident(ln_g.shape),
                  _resident(ln_b.shape), _resident(sgu_w.shape), _resident(sgu_b_t.shape),
                  _resident(conv_w.shape), _HBM],
        out_specs=row_spec,
        out_shape=jax.ShapeDtypeStruct((bsz, s, d), F32),
        scratch_shapes=_stage_scratch(*w_in.shape[1:]) + _stage_scratch(*w_out.shape[1:]) + [
            pltpu.VMEM((d_width // LANES, HALO + ODD_ROWS, LANES), F32)],
        input_output_aliases={0: 0},
        compiler_params=pltpu.CompilerParams(
            dimension_semantics=("arbitrary", "arbitrary"), vmem_limit_bytes=VMEM_LIMIT_BYTES),
        name="odd_mixer",
    )(x3d, g, w_in, ln_g, ln_b, sgu_w, sgu_b_t, conv_w, w_out)


def kernel(x, ffn1_norm, ffn1_w_gate, ffn1_w_up, ffn1_w_down, mix_norm, ffn2_norm, ffn2_w_gate, ffn2_w_up, ffn2_w_down, ev_w_in, ev_conv_w, ev_conv_b, ev_ln_g, ev_ln_b, ev_pool_w, ev_pool_scale, ev_w_out, od_w_in, od_sgu_ln_g, od_sgu_ln_b, od_sgu_w, od_sgu_b, od_conv_w, od_w_out, final_norm):
    bsz, s, d = x.shape
    depth = ffn1_norm.shape[0]
    assert s % EVEN_ROWS == 0 and s % ODD_ROWS == 0 and (bsz * s) % FFN_ROWS == 0
    assert FFN_ROWS % FFN_SUB_ROWS == 0
    assert EVEN_ROWS % MIX_SUB_ROWS == 0 and ODD_ROWS % MIX_SUB_ROWS == 0
    assert MIX_SUB_ROWS % SGU_CHUNK == 0 and MIX_SUB_ROWS % CONV_ROWS == 0
    assert ev_conv_w.shape[1] - 1 <= HALO and max(POOL_WINDOWS) <= HALO

    def row(v):
        return v.reshape(1, -1)

    fn = row(final_norm)
    ffn1 = (ffn1_norm[:, None, :], ffn1_w_gate, ffn1_w_up, ffn1_w_down)
    ffn2 = (ffn2_norm[:, None, :], ffn2_w_gate, ffn2_w_up, ffn2_w_down)
    for i in range(depth):
        j = i // 2
        x2d = _ffn(x.reshape(bsz * s, d), *ffn1, fn, layer=i, final_norm=False, in_place=i > 0)
        x = x2d.reshape(bsz, s, d)
        if i % 2 == 0:
            x = _even_mixer(x, row(mix_norm[i]), ev_w_in, ev_conv_w[j], row(ev_conv_b[j]),
                            row(ev_ln_g[j]), row(ev_ln_b[j]), ev_pool_w[j],
                            row(ev_pool_scale[j]), ev_w_out, layer=j)
        else:
            x = _odd_mixer(x, row(mix_norm[i]), od_w_in, row(od_sgu_ln_g[j]),
                           row(od_sgu_ln_b[j]), od_sgu_w[j], od_sgu_b[j].T, od_conv_w[j],
                           od_w_out, layer=j)
        x2d = _ffn(x.reshape(bsz * s, d), *ffn2, fn, layer=i, final_norm=(i == depth - 1),
                   in_place=True)
        x = x2d.reshape(bsz, s, d)
    return x
```

```python
import functools

import jax
import jax.numpy as jnp
from jax import lax
from jax.experimental import pallas as pl
from jax.experimental.pallas import tpu as pltpu

EPS = 1e-6
FFN_RES_WEIGHT = 0.5
POOL_WINDOWS = (2, 4, 8, 16)
SGU_CHUNK = 128
LANES = 128
SUBLANES = 8
BF16_ROWS = 2 * SUBLANES
HALO = 32
VMEM_LIMIT_BYTES = 56 * 1024 * 1024

FFN_ROWS = 1024
FFN_SUB_ROWS = 256
EVEN_ROWS = 1024
ODD_ROWS = 2048
MIX_SUB_ROWS = 256
CONV_ROWS = 256

BF16 = jnp.bfloat16
F32 = jnp.float32
U32 = jnp.uint32


def _rmsnorm(x, g):
    return x * lax.rsqrt(jnp.mean(x * x, axis=-1, keepdims=True) + EPS) * g


def _layernorm(x, g, b):
    mu = jnp.mean(x, axis=-1, keepdims=True)
    xc = x - mu
    var = jnp.mean(xc * xc, axis=-1, keepdims=True)
    return xc * lax.rsqrt(var + EPS) * g + b


def _dot(a, b):
    return jnp.dot(a, b, preferred_element_type=F32)


def _resident(shape):
    zeros = (0,) * len(shape)
    return pl.BlockSpec(shape, lambda *_: zeros, pipeline_mode=pl.Buffered(1))


def _layer_resident(shape, layer):
    index = (layer,) + (0,) * (len(shape) - 1)
    return pl.BlockSpec((None,) + tuple(shape[1:]), lambda *_: index,
                        pipeline_mode=pl.Buffered(1))


_HBM = pl.BlockSpec(memory_space=pl.ANY)


STAGE_CHUNKS = 8


def _stage_scratch(k, n):
    rows = k // STAGE_CHUNKS
    assert rows * STAGE_CHUNKS == k and rows % SUBLANES == 0
    return [pltpu.VMEM((k, n), BF16), pltpu.VMEM((2, rows, n), F32),
            pltpu.SemaphoreType.DMA((2,))]


def _load_weights(layer, *weights):
    def chunk_copy(w, c):
        w_hbm, _, stage, sem = w
        rows = stage.shape[1]
        return pltpu.make_async_copy(w_hbm.at[layer, pl.ds(c * rows, rows), :],
                                     stage.at[c % 2], sem.at[c % 2])

    for w in weights:
        chunk_copy(w, 0).start()
    for c in range(STAGE_CHUNKS):
        for w in weights:
            _, w_vmem, stage, _ = w
            rows = stage.shape[1]
            if c + 1 < STAGE_CHUNKS:
                chunk_copy(w, c + 1).start()
            chunk_copy(w, c).wait()
            w_vmem[c * rows:(c + 1) * rows, :] = stage[c % 2].astype(BF16)


def _ffn_kernel(x_ref, g_ref, wg_hbm, wu_hbm, wd_hbm, fn_ref, o_ref,
                wg_ref, wg_stage, wg_sem, wu_ref, wu_stage, wu_sem, wd_ref, wd_stage, wd_sem,
                *, layer, final_norm):
    @pl.when(pl.program_id(0) == 0)
    def _():
        _load_weights(layer, (wg_hbm, wg_ref, wg_stage, wg_sem),
                      (wu_hbm, wu_ref, wu_stage, wu_sem), (wd_hbm, wd_ref, wd_stage, wd_sem))

    for r0 in range(0, x_ref.shape[0], FFN_SUB_ROWS):
        rows = slice(r0, r0 + FFN_SUB_ROWS)
        x = x_ref[rows, :]
        h = _rmsnorm(x, g_ref[...]).astype(BF16)
        gate = _dot(h, wg_ref[...])
        up = _dot(h, wu_ref[...])
        act = (gate * jax.nn.sigmoid(gate) * up).astype(BF16)
        out = x + FFN_RES_WEIGHT * _dot(act, wd_ref[...])
        if final_norm:
            out = _rmsnorm(out, fn_ref[...])
        o_ref[rows, :] = out


def _ffn(x2d, g, wg, wu, wd, fn, *, layer, final_norm, in_place):
    n, d = x2d.shape
    dff = wg.shape[2]
    row_spec = pl.BlockSpec((FFN_ROWS, d), lambda i: (i, 0))
    return pl.pallas_call(
        functools.partial(_ffn_kernel, layer=layer, final_norm=final_norm),
        grid=(n // FFN_ROWS,),
        in_specs=[row_spec, _layer_resident(g.shape, layer), _HBM, _HBM, _HBM,
                  _resident(fn.shape)],
        out_specs=row_spec,
        out_shape=jax.ShapeDtypeStruct((n, d), F32),
        scratch_shapes=_stage_scratch(d, dff) + _stage_scratch(d, dff) + _stage_scratch(dff, d),
        input_output_aliases={0: 0} if in_place else {},
        compiler_params=pltpu.CompilerParams(
            dimension_semantics=("arbitrary",), vmem_limit_bytes=VMEM_LIMIT_BYTES),
        name="ffn_final" if final_norm else "ffn",
    )(x2d, g, wg, wu, wd, fn)


def _first_grid_step():
    return jnp.logical_and(pl.program_id(0) == 0, pl.program_id(1) == 0)


def _even_kernel(x_ref, g_ref, win_hbm, cw_ref, cb_ref, lg_ref, lb_ref, pw_ref, ps_ref,
                 wout_hbm, o_ref, win_ref, win_stage, win_sem, wout_ref, wout_stage, wout_sem,
                 abuf, pk_even, pk_odd, bbuf, cbuf, dbuf, *, layer):
    @pl.when(_first_grid_step())
    def _():
        _load_weights(layer, (win_hbm, win_ref, win_stage, win_sem),
                      (wout_hbm, wout_ref, wout_stage, wout_sem))

    rows = x_ref.shape[0]
    taps, lane_tiles, pack_rows, _ = cw_ref.shape
    a_width = lane_tiles * LANES
    n_groups, _, group_dim = bbuf.shape
    ext_rows = HALO + rows

    @pl.when(pl.program_id(1) == 0)
    def _():
        abuf[:, 0:HALO, :] = jnp.zeros((lane_tiles, HALO, LANES), F32)
        abuf[:, ext_rows:ext_rows + SUBLANES, :] = jnp.zeros((lane_tiles, SUBLANES, LANES), F32)
        bbuf[:, 0:HALO, :] = jnp.zeros((n_groups, HALO, group_dim), F32)

    for r0 in range(0, rows, MIX_SUB_ROWS):
        h = _rmsnorm(x_ref[r0:r0 + MIX_SUB_ROWS, :], g_ref[...]).astype(BF16)
        z = _dot(h, win_ref[...])
        a_val = z[:, :a_width]
        a_gate = z[:, a_width:2 * a_width]
        glu = a_val * jax.nn.sigmoid(a_gate)
        for j in range(lane_tiles):
            abuf[j, HALO + r0:HALO + r0 + MIX_SUB_ROWS, :] = glu[:, j * LANES:(j + 1) * LANES]
        for gi in range(n_groups):
            c0 = 2 * a_width + gi * group_dim
            bbuf[gi, HALO + r0:HALO + r0 + MIX_SUB_ROWS, :] = z[:, c0:c0 + group_dim]

    for j in range(lane_tiles):
        pk_even[j] = pltpu.bitcast(abuf[j, 0:ext_rows, :].astype(BF16), U32)
        pk_odd[j] = pltpu.bitcast(abuf[j, 1:ext_rows + 1, :].astype(BF16), U32)
    first = HALO - (taps - 1)
    words = pack_rows // 2

    def conv_lane_tile(j, carry):
        for blk in range(rows // CONV_ROWS):
            accs = [jnp.zeros((pack_rows, LANES), F32) for _ in range(CONV_ROWS // pack_rows)]
            for k in range(taps):
                off = blk * CONV_ROWS + first + k
                src = pk_odd if off % 2 else pk_even
                wk = cw_ref[k, j].astype(F32)
                for gi in range(len(accs)):
                    w0 = off // 2 + gi * words
                    xk = pltpu.bitcast(src[j, w0:w0 + words, :], BF16)
                    accs[gi] = accs[gi] + xk.astype(F32) * wk
            for gi, acc in enumerate(accs):
                q0 = blk * CONV_ROWS + gi * pack_rows
                cbuf[j, q0:q0 + pack_rows, :] = acc
        return carry

    lax.fori_loop(0, lane_tiles, conv_lane_tile, 0)

    pos = lax.broadcasted_iota(jnp.int32, (CONV_ROWS, LANES), 0) + pl.program_id(1) * rows + 1
    pair = 2 * group_dim
    for r0 in range(0, rows, MIX_SUB_ROWS):
        conv = jnp.concatenate([cbuf[j, r0:r0 + MIX_SUB_ROWS, :] for j in range(lane_tiles)],
                               axis=-1)
        a = _layernorm(conv + cb_ref[...], lg_ref[...], lb_ref[...])
        a = a * jax.nn.sigmoid(a)
        for gi, win in enumerate(POOL_WINDOWS):
            lanes = slice(gi * group_dim, (gi + 1) * group_dim)
            for q0 in range(r0, r0 + MIX_SUB_ROWS, CONV_ROWS):
                v = bbuf[gi, HALO + q0:HALO + q0 + CONV_ROWS, :]
                wsum = v
                for back in range(1, win):
                    wsum = wsum + bbuf[gi, HALO + q0 - back:HALO + q0 - back + CONV_ROWS, :]
                if q0 == 0:
                    mean = wsum / jnp.minimum(pos, win).astype(F32)
                else:
                    mean = wsum * (1.0 / win)
                dbuf[q0:q0 + CONV_ROWS, lanes] = (mean - v).astype(BF16)
        b_parts = [_dot(dbuf[r0:r0 + MIX_SUB_ROWS, p * pair:(p + 1) * pair], pw_ref[p])
                   for p in range(n_groups // 2)]
        b = jnp.concatenate(b_parts, axis=-1) * ps_ref[...]
        mixed = jnp.concatenate([a, b], axis=-1).astype(BF16)
        o_ref[r0:r0 + MIX_SUB_ROWS, :] = (x_ref[r0:r0 + MIX_SUB_ROWS, :]
                                          + _dot(mixed, wout_ref[...]))

    abuf[:, 0:HALO, :] = abuf[:, rows:ext_rows, :]
    bbuf[:, 0:HALO, :] = bbuf[:, rows:ext_rows, :]


def _even_mixer(x3d, g, w_in, conv_w, conv_b, ln_g, ln_b, pool_w, pool_scale, w_out, *, layer):
    bsz, s, d = x3d.shape
    taps, a_width = conv_w.shape
    b_width = pool_scale.shape[1]
    n_groups, group_dim, _ = pool_w.shape
    lane_tiles = a_width // LANES
    conv_wb = jnp.broadcast_to(conv_w.reshape(taps, lane_tiles, 1, LANES),
                               (taps, lane_tiles, BF16_ROWS, LANES)).astype(BF16)
    pw = pool_w.reshape(n_groups // 2, 2, group_dim, group_dim)
    zero = jnp.zeros_like(pw[:, 0])
    pool_wb = jnp.concatenate(
        [jnp.concatenate([pw[:, 0], zero], axis=-1), jnp.concatenate([zero, pw[:, 1]], axis=-1)],
        axis=-2).astype(BF16)
    row_spec = pl.BlockSpec((None, EVEN_ROWS, d), lambda b, t: (b, t, 0))
    ext_rows = HALO + EVEN_ROWS
    return pl.pallas_call(
        functools.partial(_even_kernel, layer=layer),
        grid=(bsz, s // EVEN_ROWS),
        in_specs=[row_spec, _resident(g.shape), _HBM, _resident(conv_wb.shape),
                  _resident(conv_b.shape), _resident(ln_g.shape), _resident(ln_b.shape),
                  _resident(pool_wb.shape), _resident(pool_scale.shape), _HBM],
        out_specs=row_spec,
        out_shape=jax.ShapeDtypeStruct((bsz, s, d), F32),
        scratch_shapes=_stage_scratch(*w_in.shape[1:]) + _stage_scratch(*w_out.shape[1:]) + [
            pltpu.VMEM((lane_tiles, ext_rows + SUBLANES, LANES), F32),
            pltpu.VMEM((lane_tiles, ext_rows // 2, LANES), U32),
            pltpu.VMEM((lane_tiles, ext_rows // 2, LANES), U32),
            pltpu.VMEM((n_groups, ext_rows, group_dim), F32),
            pltpu.VMEM((lane_tiles, EVEN_ROWS, LANES), F32),
            pltpu.VMEM((EVEN_ROWS, b_width), BF16)],
        input_output_aliases={0: 0},
        compiler_params=pltpu.CompilerParams(
            dimension_semantics=("arbitrary", "arbitrary"), vmem_limit_bytes=VMEM_LIMIT_BYTES),
        name="even_mixer",
    )(x3d, g, w_in, conv_wb, conv_b, ln_g, ln_b, pool_wb, pool_scale, w_out)


def _gelu_tanh(x):
    c = 0.7978845608028654
    return 0.5 * x * (1.0 + jnp.tanh(c * (x + 0.044715 * (x * x * x))))


def _odd_kernel(x_ref, g_ref, win_hbm, lg_ref, lb_ref, sw_ref, sb_ref, cw_ref, wout_hbm,
                o_ref, win_ref, win_stage, win_sem, wout_ref, wout_stage, wout_sem, dbuf,
                *, layer):
    @pl.when(_first_grid_step())
    def _():
        _load_weights(layer, (win_hbm, win_ref, win_stage, win_sem),
                      (wout_hbm, wout_ref, wout_stage, wout_sem))

    rows = x_ref.shape[0]
    c_width = lg_ref.shape[1]
    d_width = cw_ref.shape[1]
    taps = cw_ref.shape[0]
    n_groups = sw_ref.shape[0]
    group_dim = c_width // n_groups
    lane_tiles = d_width // LANES

    @pl.when(pl.program_id(1) == 0)
    def _():
        dbuf[:, 0:HALO, :] = jnp.zeros((lane_tiles, HALO, LANES), F32)

    tri = (lax.broadcasted_iota(jnp.int32, (SGU_CHUNK, SGU_CHUNK), 0)
           >= lax.broadcasted_iota(jnp.int32, (SGU_CHUNK, SGU_CHUNK), 1))
    w_s = [jnp.where(tri, sw_ref[gi], 0.0).astype(BF16) for gi in range(n_groups)]

    for r0 in range(0, rows, MIX_SUB_ROWS):
        x = x_ref[r0:r0 + MIX_SUB_ROWS, :]
        h = _rmsnorm(x, g_ref[...]).astype(BF16)
        z = _dot(h, win_ref[...])
        c_u = _gelu_tanh(z[:, :c_width])
        c_v = _layernorm(_gelu_tanh(z[:, c_width:2 * c_width]), lg_ref[...],
                         lb_ref[...]).astype(BF16)
        off = 2 * c_width
        d_b = z[:, off:off + d_width]
        d_cx = z[:, off + d_width:off + 2 * d_width] * z[:, off + 2 * d_width:]
        for j in range(lane_tiles):
            dbuf[j, HALO + r0:HALO + r0 + MIX_SUB_ROWS, :] = d_cx[:, j * LANES:(j + 1) * LANES]

        chunk_rows = []
        for n in range(MIX_SUB_ROWS // SGU_CHUNK):
            parts = []
            for gi in range(n_groups):
                v = c_v[n * SGU_CHUNK:(n + 1) * SGU_CHUNK, gi * group_dim:(gi + 1) * group_dim]
                parts.append(_dot(w_s[gi], v) + sb_ref[:, gi:gi + 1])
            chunk_rows.append(jnp.concatenate(parts, axis=-1))
        c_out = c_u * jnp.concatenate(chunk_rows, axis=0)

        conv_tiles = []
        for j in range(lane_tiles):
            conv = jnp.zeros((MIX_SUB_ROWS, LANES), F32)
            for k in range(taps):
                start = HALO + r0 - (taps - 1) + k
                conv = conv + (cw_ref[k:k + 1, j * LANES:(j + 1) * LANES]
                               * dbuf[j, start:start + MIX_SUB_ROWS, :])
            conv_tiles.append(conv)
        d_out = d_b * jnp.concatenate(conv_tiles, axis=-1)

        mixed = jnp.concatenate([c_out, d_out], axis=-1).astype(BF16)
        o_ref[r0:r0 + MIX_SUB_ROWS, :] = x + _dot(mixed, wout_ref[...])

    dbuf[:, 0:HALO, :] = dbuf[:, rows:rows + HALO, :]


def _odd_mixer(x3d, g, w_in, ln_g, ln_b, sgu_w, sgu_b_t, conv_w, w_out, *, layer):
    bsz, s, d = x3d.shape
    d_width = conv_w.shape[1]
    row_spec = pl.BlockSpec((None, ODD_ROWS, d), lambda b, t: (b, t, 0))
    return pl.pallas_call(
        functools.partial(_odd_kernel, layer=layer),
        grid=(bsz, s // ODD_ROWS),
        in_specs=[row_spec, _resident(g.shape), _HBM, _resident(ln_g.shape),
                  _resident(ln_b.shape), _resident(sgu_w.shape), _resident(sgu_b_t.shape),
                  _resident(conv_w.shape), _HBM],
        out_specs=row_spec,
        out_shape=jax.ShapeDtypeStruct((bsz, s, d), F32),
        scratch_shapes=_stage_scratch(*w_in.shape[1:]) + _stage_scratch(*w_out.shape[1:]) + [
            pltpu.VMEM((d_width // LANES, HALO + ODD_ROWS, LANES), F32)],
        input_output_aliases={0: 0},
        compiler_params=pltpu.CompilerParams(
            dimension_semantics=("arbitrary", "arbitrary"), vmem_limit_bytes=VMEM_LIMIT_BYTES),
        name="odd_mixer",
    )(x3d, g, w_in, ln_g, ln_b, sgu_w, sgu_b_t, conv_w, w_out)


def kernel(x, ffn1_norm, ffn1_w_gate, ffn1_w_up, ffn1_w_down, mix_norm, ffn2_norm, ffn2_w_gate, ffn2_w_up, ffn2_w_down, ev_w_in, ev_conv_w, ev_conv_b, ev_ln_g, ev_ln_b, ev_pool_w, ev_pool_scale, ev_w_out, od_w_in, od_sgu_ln_g, od_sgu_ln_b, od_sgu_w, od_sgu_b, od_conv_w, od_w_out, final_norm):
    bsz, s, d = x.shape
    depth = ffn1_norm.shape[0]
    assert s % EVEN_ROWS == 0 and s % ODD_ROWS == 0 and (bsz * s) % FFN_ROWS == 0
    assert FFN_ROWS % FFN_SUB_ROWS == 0
    assert EVEN_ROWS % MIX_SUB_ROWS == 0 and ODD_ROWS % MIX_SUB_ROWS == 0
    assert MIX_SUB_ROWS % SGU_CHUNK == 0 and MIX_SUB_ROWS % CONV_ROWS == 0
    assert ev_conv_w.shape[1] - 1 <= HALO and max(POOL_WINDOWS) <= HALO

    def row(v):
        return v.reshape(1, -1)

    fn = row(final_norm)
    ffn1 = (ffn1_norm[:, None, :], ffn1_w_gate, ffn1_w_up, ffn1_w_down)
    ffn2 = (ffn2_norm[:, None, :], ffn2_w_gate, ffn2_w_up, ffn2_w_down)
    for i in range(depth):
        j = i // 2
        x2d = _ffn(x.reshape(bsz * s, d), *ffn1, fn, layer=i, final_norm=False, in_place=i > 0)
        x = x2d.reshape(bsz, s, d)
        if i % 2 == 0:
            x = _even_mixer(x, row(mix_norm[i]), ev_w_in, ev_conv_w[j], row(ev_conv_b[j]),
                            row(ev_ln_g[j]), row(ev_ln_b[j]), ev_pool_w[j],
                            row(ev_pool_scale[j]), ev_w_out, layer=j)
        else:
            x = _odd_mixer(x, row(mix_norm[i]), od_w_in, row(od_sgu_ln_g[j]),
                           row(od_sgu_ln_b[j]), od_sgu_w[j], od_sgu_b[j].T, od_conv_w[j],
                           od_w_out, layer=j)
        x2d = _ffn(x.reshape(bsz * s, d), *ffn2, fn, layer=i, final_norm=(i == depth - 1),
                   in_place=True)
        x = x2d.reshape(bsz, s, d)
    return x
```
